```python
import jax, jax.numpy as jnp
from jax import lax
import numpy as np

D_MODEL = 1024
BATCH = 4
SEQ = 8192
DEPTH = 1

SSD_EXPAND = 2
SSD_D_INNER = SSD_EXPAND * D_MODEL
SSD_HEAD_DIM = 64
SSD_N_HEADS = SSD_D_INNER // SSD_HEAD_DIM
SSD_N_GROUPS = 4
SSD_D_STATE = 128
SSD_CONV = 4
SSD_CHUNK = 128
SSD_CONV_DIM = SSD_D_INNER + 2 * SSD_N_GROUPS * SSD_D_STATE

ATTN_HEAD_DIM = 64
ATTN_N_HEADS = 16
ATTN_WIDTH = ATTN_N_HEADS * ATTN_HEAD_DIM
MOBA_BLOCK = 256
MOBA_TOPK = 3
MOBA_Q_CHUNK = 32

FFN_HIDDEN = 4 * D_MODEL
FFN_CONV = 3

NORM_EPS = 1e-6

IN_SIZES = (SSD_D_INNER, SSD_CONV_DIM, SSD_N_HEADS, ATTN_WIDTH, ATTN_WIDTH, ATTN_WIDTH, D_MODEL, D_MODEL)
IN_COLS = sum(IN_SIZES)
IN_OFFSETS = tuple(sum(IN_SIZES[:j]) for j in range(1, len(IN_SIZES)))

kernel_name = 'hybrid_ssd_moba_block'


def rms_norm(x, w):
    xf = x.astype(jnp.float32)
    y = xf * lax.rsqrt(jnp.mean(xf * xf, axis=-1, keepdims=True) + NORM_EPS)
    return (y * w.astype(jnp.float32)).astype(x.dtype)


def group_rms_norm(x, w, groups):
    shp = x.shape
    xf = x.astype(jnp.float32).reshape(shp[:-1] + (groups, shp[-1] // groups))
    y = xf * lax.rsqrt(jnp.mean(xf * xf, axis=-1, keepdims=True) + NORM_EPS)
    return (y.reshape(shp) * w.astype(jnp.float32)).astype(x.dtype)


def causal_dwconv(x, w, b):
    width, ch = w.shape
    y = lax.conv_general_dilated(x, w[:, None, :].astype(x.dtype), window_strides=(1,),
                                 padding=[(width - 1, 0)],
                                 dimension_numbers=('NWC', 'WIO', 'NWC'),
                                 feature_group_count=ch)
    return y + b.astype(x.dtype)


def ssd_chunked_scan(x, dt, a, bm, cm):
    b, s, h, p = x.shape
    g, n = bm.shape[-2:]
    r = h // g
    nc = s // SSD_CHUNK
    xdt = (x * dt[..., None]).reshape(b, nc, SSD_CHUNK, g, r, p)
    adt = (dt * a).reshape(b, nc, SSD_CHUNK, g, r)
    bc = bm.reshape(b, nc, SSD_CHUNK, g, n)
    cc = cm.reshape(b, nc, SSD_CHUNK, g, n)
    xs = (jnp.moveaxis(xdt, 1, 0), jnp.moveaxis(adt, 1, 0), jnp.moveaxis(bc, 1, 0), jnp.moveaxis(cc, 1, 0))
    causal = jnp.tril(jnp.ones((SSD_CHUNK, SSD_CHUNK), dtype=bool))

    def step(state, inp):
        xc, ac, bch, cch = inp
        acs = jnp.cumsum(ac, axis=1)
        seg = acs[:, :, None] - acs[:, None, :]
        decay = jnp.exp(jnp.where(causal[None, :, :, None, None], seg, -jnp.inf))
        cb = jnp.einsum('blgn,bsgn->blsg', cch, bch)
        y = jnp.einsum('blsg,blsgr,bsgrp->blgrp', cb, decay, xc)
        y = y + jnp.einsum('blgn,bgrpn->blgrp', cch, state) * jnp.exp(acs)[..., None]
        last = acs[:, -1]
        w_end = jnp.exp(last[:, None] - acs)
        state = state * jnp.exp(last)[..., None, None] + jnp.einsum('blgn,blgr,blgrp->bgrpn', bch, w_end, xc)
        return state, y

    state0 = jnp.zeros((b, g, r, p, n), jnp.float32)
    _, ys = lax.scan(step, state0, xs)
    return jnp.moveaxis(ys, 0, 1).reshape(b, s, h, p)


def alibi_slopes(n_heads):
    return 2.0 ** (-8.0 * jnp.arange(1, n_heads + 1, dtype=jnp.float32) / n_heads)


def moba_attention(q, k, v):
    b, s, h, dh = q.shape
    sp = -(-s // MOBA_BLOCK) * MOBA_BLOCK
    nb = sp // MOBA_BLOCK
    k_sel = min(MOBA_TOPK, nb)
    pad = [(0, 0), (0, sp - s), (0, 0), (0, 0)]
    qp = jnp.pad(q, pad).transpose(0, 2, 1, 3)
    kb = jnp.pad(k, pad).transpose(0, 2, 1, 3).reshape(b, h, nb, MOBA_BLOCK, dh)
    vb = jnp.pad(v, pad).transpose(0, 2, 1, 3).reshape(b, h, nb, MOBA_BLOCK, dh)
    kmean = jnp.mean(kb.astype(jnp.float32), axis=3)
    slopes = alibi_slopes(h)
    scale = dh ** -0.5
    offs = jnp.arange(MOBA_BLOCK)
    gather_blocks = jax.vmap(jax.vmap(lambda blocks, idx: blocks[idx]))

    def chunk(c):
        q0 = c * MOBA_Q_CHUNK
        qc = lax.dynamic_slice_in_dim(qp, q0, MOBA_Q_CHUNK, axis=2).astype(jnp.float32)
        qpos = q0 + jnp.arange(MOBA_Q_CHUNK)
        own = q0 // MOBA_BLOCK
        gate = jnp.einsum('bhqd,bhnd->bhqn', qc, kmean)
        gate = jnp.where(jnp.arange(nb) < own, gate, -jnp.inf)
        top_s, top_i = lax.top_k(gate, k_sel)
        ksel = gather_blocks(kb, top_i).astype(jnp.float32)
        vsel = gather_blocks(vb, top_i).astype(jnp.float32)
        kpos = top_i[..., None] * MOBA_BLOCK + offs
        s_sel = (jnp.einsum('bhqd,bhqjsd->bhqjs', qc, ksel) * scale
                 - slopes[None, :, None, None, None] * (qpos[:, None, None] - kpos).astype(jnp.float32))
        s_sel = jnp.where(jnp.isfinite(top_s)[..., None], s_sel, -jnp.inf)
        ko = lax.dynamic_index_in_dim(kb, own, axis=2, keepdims=False).astype(jnp.float32)
        vo = lax.dynamic_index_in_dim(vb, own, axis=2, keepdims=False).astype(jnp.float32)
        kpos_o = own * MOBA_BLOCK + offs
        s_own = (jnp.einsum('bhqd,bhsd->bhqs', qc, ko) * scale
                 - slopes[None, :, None, None] * (qpos[:, None] - kpos_o[None, :]).astype(jnp.float32))
        s_own = jnp.where(kpos_o[None, :] <= qpos[:, None], s_own, -jnp.inf)
        n_sel = k_sel * MOBA_BLOCK
        scores = jnp.concatenate([s_sel.reshape(b, h, MOBA_Q_CHUNK, n_sel), s_own], axis=-1)
        prob = jax.nn.softmax(scores, axis=-1)
        p_sel = prob[..., :n_sel].reshape(b, h, MOBA_Q_CHUNK, k_sel, MOBA_BLOCK)
        p_own = prob[..., n_sel:]
        out = jnp.einsum('bhqjs,bhqjsd->bhqd', p_sel, vsel) + jnp.einsum('bhqs,bhsd->bhqd', p_own, vo)
        return out.astype(q.dtype)

    outs = lax.map(chunk, jnp.arange(sp // MOBA_Q_CHUNK))
    out = outs.transpose(1, 0, 3, 2, 4).reshape(b, sp, h, dh)
    return out[:, :s]


def setup_inputs(seed: int = 0) -> dict:
    key = jax.random.key(seed)
    ks = jax.random.split(key, 24)
    f32 = jnp.float32

    def dense(k, fan_in, fan_out):
        return jax.random.normal(k, (DEPTH, fan_in, fan_out), f32) * fan_in ** -0.5

    def gain(k, n):
        return 1.0 + 0.05 * jax.random.normal(k, (DEPTH, n), f32)

    dt0 = jnp.exp(jax.random.uniform(ks[5], (DEPTH, SSD_N_HEADS), f32) * (np.log(0.1) - np.log(0.001)) + np.log(0.001))
    dt_bias = dt0 + jnp.log(-jnp.expm1(-dt0))
    return {
        'x': jax.random.normal(ks[0], (BATCH, SEQ, D_MODEL), f32),
        'pre_mix_norm': gain(ks[1], D_MODEL),
        'w_in': dense(ks[2], D_MODEL, IN_COLS),
        'ssd_conv_w': jax.random.normal(ks[3], (DEPTH, SSD_CONV, SSD_CONV_DIM), f32) * SSD_CONV ** -0.5,
        'ssd_conv_b': 0.01 * jax.random.normal(ks[4], (DEPTH, SSD_CONV_DIM), f32),
        'ssd_dt_bias': dt_bias,
        'ssd_a_log': jnp.log(jax.random.uniform(ks[6], (DEPTH, SSD_N_HEADS), f32, 1.0, 16.0)),
        'ssd_d_skip': 1.0 + 0.1 * jax.random.normal(ks[7], (DEPTH, SSD_N_HEADS), f32),
        'ssd_out_norm': gain(ks[8], SSD_D_INNER),
        'w_ssd_branch': dense(ks[9], SSD_D_INNER, D_MODEL),
        'w_attn_branch': dense(ks[10], ATTN_WIDTH, D_MODEL),
        'w_out': dense(ks[11], D_MODEL, D_MODEL),
        'post_mix_norm': gain(ks[12], D_MODEL),
        'pre_ffn_norm': gain(ks[13], D_MODEL),
        'w_ffn_up': dense(ks[14], D_MODEL, 2 * FFN_HIDDEN),
        'ffn_conv_w': jax.random.normal(ks[15], (DEPTH, FFN_CONV, FFN_HIDDEN), f32) * FFN_CONV ** -0.5,
        'ffn_conv_b': 0.01 * jax.random.normal(ks[16], (DEPTH, FFN_HIDDEN), f32),
        'w_ffn_down': dense(ks[17], FFN_HIDDEN, D_MODEL),
        'post_ffn_norm': gain(ks[18], D_MODEL),
    }


def reference(x, pre_mix_norm, w_in, ssd_conv_w, ssd_conv_b, ssd_dt_bias, ssd_a_log, ssd_d_skip,
              ssd_out_norm, w_ssd_branch, w_attn_branch, w_out, post_mix_norm, pre_ffn_norm,
              w_ffn_up, ffn_conv_w, ffn_conv_b, w_ffn_down, post_ffn_norm):
    b, s, _ = x.shape
    gn = SSD_N_GROUPS * SSD_D_STATE
    for i in range(DEPTH):
        h = rms_norm(x, pre_mix_norm[i])
        proj = h @ w_in[i]
        z, xbc, dt_raw, q, k, v, g_ssd, g_attn = jnp.split(proj, IN_OFFSETS, axis=-1)

        xbc = jax.nn.silu(causal_dwconv(xbc, ssd_conv_w[i], ssd_conv_b[i]))
        xs, bm, cm = jnp.split(xbc, [SSD_D_INNER, SSD_D_INNER + gn], axis=-1)
        xs4 = xs.reshape(b, s, SSD_N_HEADS, SSD_HEAD_DIM).astype(jnp.float32)
        dt = jax.nn.softplus((dt_raw + ssd_dt_bias[i]).astype(jnp.float32))
        a = -jnp.exp(ssd_a_log[i].astype(jnp.float32))
        y = ssd_chunked_scan(xs4, dt, a,
                             bm.reshape(b, s, SSD_N_GROUPS, SSD_D_STATE).astype(jnp.float32),
                             cm.reshape(b, s, SSD_N_GROUPS, SSD_D_STATE).astype(jnp.float32))
        y = y + ssd_d_skip[i].astype(jnp.float32)[:, None] * xs4
        y = y.reshape(b, s, SSD_D_INNER).astype(x.dtype)
        y = group_rms_norm(y * jax.nn.silu(z), ssd_out_norm[i], SSD_N_GROUPS)
        y_ssd = y @ w_ssd_branch[i]

        att = moba_attention(q.reshape(b, s, ATTN_N_HEADS, ATTN_HEAD_DIM),
                             k.reshape(b, s, ATTN_N_HEADS, ATTN_HEAD_DIM),
                             v.reshape(b, s, ATTN_N_HEADS, ATTN_HEAD_DIM))
        y_attn = att.reshape(b, s, ATTN_WIDTH) @ w_attn_branch[i]

        mixed = jax.nn.sigmoid(g_ssd) * y_ssd + jax.nn.sigmoid(g_attn) * y_attn
        x = x + rms_norm(mixed @ w_out[i], post_mix_norm[i])

        h = rms_norm(x, pre_ffn_norm[i])
        gate, up = jnp.split(h @ w_ffn_up[i], 2, axis=-1)
        gate = causal_dwconv(gate, ffn_conv_w[i], ffn_conv_b[i])
        ff = (jax.nn.gelu(gate, approximate=True) * up) @ w_ffn_down[i]
        x = x + rms_norm(ff, post_ffn_norm[i])
    return x
```

```python
import functools

import jax
import jax.numpy as jnp
from jax import lax
from jax.experimental import pallas as pl
from jax.experimental.pallas import tpu as pltpu

F32 = jnp.float32
BF16 = jnp.bfloat16

D_MODEL = 1024
SSD_D_INNER = 2048
SSD_HEAD_DIM = 64
SSD_N_HEADS = 32
SSD_N_GROUPS = 4
SSD_D_STATE = 128
SSD_CONV = 4
SSD_CHUNK = 128
SSD_GN = SSD_N_GROUPS * SSD_D_STATE
SSD_GROUP_W = SSD_D_INNER // SSD_N_GROUPS
ATTN_HEAD_DIM = 64
ATTN_N_HEADS = 16
ATTN_WIDTH = 1024
MOBA_BLOCK = 256
MOBA_TOPK = 3
FFN_HIDDEN = 4096
FFN_CONV = 3
NORM_EPS = 1e-6

LANES = 128
HEADS_PER_LANE_TILE = LANES // ATTN_HEAD_DIM
NEG_BIG = -1e30

_IN_SIZES = (SSD_D_INNER, SSD_D_INNER + 2 * SSD_GN, SSD_N_HEADS, ATTN_WIDTH, ATTN_WIDTH, ATTN_WIDTH,
             D_MODEL, D_MODEL)
_IN_STARTS = tuple(sum(_IN_SIZES[:j]) for j in range(len(_IN_SIZES)))
Z_OFF = 0
XS_OFF = 2048
B_OFF = 4096
C_OFF = 4608
Q_OFF = 5120
K_OFF = 6144
GS_OFF = 7168
GA_OFF = 8192
PROJ_COLS = 9216

VMEM_LIMIT = 48 * 1024 * 1024


def _dot(a, b):
    return jnp.dot(a, b, preferred_element_type=F32)


def _dot_nt(a, b):
    return lax.dot_general(a, b, (((1,), (1,)), ((), ())), preferred_element_type=F32)


def _rms(x, w):
    return x * lax.rsqrt(jnp.mean(x * x, axis=-1, keepdims=True) + NORM_EPS) * w


def _sigmoid(x):
    return 1.0 / (1.0 + jnp.exp(-x))


def _silu(x):
    return x * _sigmoid(x)


def _split3(v):
    hi = v.astype(BF16)
    r1 = v - hi.astype(F32)
    mid = r1.astype(BF16)
    lo = (r1 - mid.astype(F32)).astype(BF16)
    return hi, mid, lo


def _shift_rows(x, k, carry):
    r = pltpu.roll(x, k, axis=0)
    row = lax.broadcasted_iota(jnp.int32, x.shape, 0)
    hist = carry.shape[0]
    for t in range(k):
        r = jnp.where(row == t, carry[hist - k + t:hist - k + t + 1, :], r)
    return r


def _inproj_kernel(x_ref, nw_ref, w_ref, wdt_ref, proj_ref, dt_ref, h_ref):
    @pl.when(pl.program_id(1) == 0)
    def _():
        h = _rms(x_ref[...], nw_ref[...]).astype(BF16)
        h_ref[...] = h
        dt_ref[...] = _dot(h, wdt_ref[...])

    proj_ref[...] = _dot(h_ref[...], w_ref[...]).astype(BF16)


def _inproj(x2, nw, wp, wdt, tm=1024, tn=1024):
    t = x2.shape[0]
    return pl.pallas_call(
        _inproj_kernel,
        grid=(t // tm, PROJ_COLS // tn),
        in_specs=[
            pl.BlockSpec((tm, D_MODEL), lambda i, j: (i, 0)),
            pl.BlockSpec((1, D_MODEL), lambda i, j: (0, 0)),
            pl.BlockSpec((D_MODEL, tn), lambda i, j: (0, j)),
            pl.BlockSpec((D_MODEL, LANES), lambda i, j: (0, 0)),
        ],
        out_specs=[
            pl.BlockSpec((tm, tn), lambda i, j: (i, j)),
            pl.BlockSpec((tm, LANES), lambda i, j: (i, 0)),
        ],
        out_shape=[
            jax.ShapeDtypeStruct((t, PROJ_COLS), BF16),
            jax.ShapeDtypeStruct((t, LANES), F32),
        ],
        scratch_shapes=[pltpu.VMEM((tm, D_MODEL), BF16)],
        compiler_params=pltpu.CompilerParams(
            dimension_semantics=("arbitrary", "arbitrary"), vmem_limit_bytes=VMEM_LIMIT),
        name="inproj",
    )(x2, nw, wp, wdt)


def _vt_kernel(x_ref, nw_ref, wvt_ref, vt_ref):
    h = _rms(x_ref[...], nw_ref[...]).astype(BF16)
    vt = _dot_nt(wvt_ref[...], h)
    for c in range(vt_ref.shape[0]):
        vt_ref[c] = vt[:, c * MOBA_BLOCK:(c + 1) * MOBA_BLOCK].astype(BF16)


def _vproj_t(x2, nw, wvt, tm=512):
    t = x2.shape[0]
    return pl.pallas_call(
        _vt_kernel,
        grid=(t // tm,),
        in_specs=[
            pl.BlockSpec((tm, D_MODEL), lambda i: (i, 0)),
            pl.BlockSpec((1, D_MODEL), lambda i: (0, 0)),
            pl.BlockSpec((ATTN_WIDTH, D_MODEL), lambda i: (0, 0)),
        ],
        out_specs=pl.BlockSpec((tm // MOBA_BLOCK, ATTN_WIDTH, MOBA_BLOCK), lambda i: (i, 0, 0)),
        out_shape=jax.ShapeDtypeStruct((t // MOBA_BLOCK, ATTN_WIDTH, MOBA_BLOCK), BF16),
        compiler_params=pltpu.CompilerParams(
            dimension_semantics=("arbitrary",), vmem_limit_bytes=VMEM_LIMIT),
        name="vproj_t",
    )(x2, nw, wvt)


def _moba_kernel(slopes_ref, q_ref, k_ref, vt_ref, o_ref, km_ref, sb_ref):
    hp = pl.program_id(1)
    i = pl.program_id(2)
    nb = km_ref.shape[0]
    blk = MOBA_BLOCK

    @pl.when(i == 0)
    def _():
        km_ref[...] = jnp.zeros_like(km_ref)

    qs = q_ref[...] * ATTN_HEAD_DIM ** -0.5
    lane = lax.broadcasted_iota(jnp.int32, (1, LANES), 1)
    key_pos = lax.broadcasted_iota(jnp.int32, (blk, blk), 0)
    qry_pos = lax.broadcasted_iota(jnp.int32, (blk, blk), 1)
    blk_id = lax.broadcasted_iota(jnp.int32, (nb, blk), 0)
    own_start = pl.multiple_of(i * blk, blk)
    k_own = k_ref[pl.ds(own_start, blk), :]
    vt_own = vt_ref[i]
    km_hi, km_mid, km_lo = _split3(km_ref[...])
    valid = blk_id < i

    outs = []
    for hh in range(HEADS_PER_LANE_TILE):
        lo_lane = hh * ATTN_HEAD_DIM
        qm = jnp.where((lane >= lo_lane) & (lane < lo_lane + ATTN_HEAD_DIM), qs, jnp.zeros_like(qs))
        vrows = slice(lo_lane, lo_lane + ATTN_HEAD_DIM)
        slope = slopes_ref[hp * HEADS_PER_LANE_TILE + hh]

        gate = _dot_nt(km_hi, qm) + _dot_nt(km_mid, qm) + _dot_nt(km_lo, qm)
        g = jnp.where(valid, gate, -jnp.inf)
        cnt = jnp.zeros((nb, blk), jnp.int32)
        for jp in range(nb):
            gj = g[jp:jp + 1, :]
            beats = (gj > g) | ((gj == g) & (blk_id > jp))
            cnt = cnt + beats.astype(jnp.int32)
        sel = (cnt < MOBA_TOPK) & valid
        sb_ref[hh] = jnp.where(sel, 0.0, NEG_BIG)

        col_bias = slope * key_pos.astype(F32)

        s = _dot_nt(k_own, qm) + col_bias
        s = jnp.where(key_pos <= qry_pos, s, NEG_BIG)
        m = jnp.max(s, axis=0, keepdims=True)
        p = jnp.exp(s - m)
        l = jnp.sum(p, axis=0, keepdims=True)
        acc = _dot(vt_own[vrows, :], p.astype(BF16))

        def body(j, carry, qm=qm, hh=hh, vrows=vrows, slope=slope, col_bias=col_bias):
            m, l, acc = carry
            kj = k_ref[pl.ds(pl.multiple_of(j * blk, blk), blk), :]
            vtj = vt_ref[j]
            row_bias = sb_ref[hh, pl.ds(j, 1), :] + slope * (blk * (j - i)).astype(F32)
            s = _dot_nt(kj, qm) + col_bias + row_bias
            m_new = jnp.maximum(m, jnp.max(s, axis=0, keepdims=True))
            alpha = jnp.exp(m - m_new)
            p = jnp.exp(s - m_new)
            l = alpha * l + jnp.sum(p, axis=0, keepdims=True)
            acc = alpha * acc + _dot(vtj[vrows, :], p.astype(BF16))
            return m_new, l, acc

        m, l, acc = lax.fori_loop(0, i, body, (m, l, acc))
        outs.append(acc / l)

    o_ref[...] = jnp.concatenate(outs, axis=0).T.astype(o_ref.dtype)
    km_ref[pl.ds(i, 1), :] = jnp.mean(k_own.astype(F32), axis=0, keepdims=True)


def _moba(slopes, proj, vt3, batch, seq):
    t = proj.shape[0]
    nq = seq // MOBA_BLOCK
    n_pairs = ATTN_N_HEADS // HEADS_PER_LANE_TILE
    q_blk0 = Q_OFF // LANES
    k_blk0 = K_OFF // LANES
    return pl.pallas_call(
        _moba_kernel,
        grid=(batch, n_pairs, nq),
        in_specs=[
            pl.BlockSpec(memory_space=pltpu.SMEM),
            pl.BlockSpec((MOBA_BLOCK, LANES), lambda b, hp, i: (b * nq + i, q_blk0 + hp)),
            pl.BlockSpec((seq, LANES), lambda b, hp, i: (b, k_blk0 + hp)),
            pl.BlockSpec((nq, LANES, MOBA_BLOCK), lambda b, hp, i: (b, hp, 0)),
        ],
        out_specs=pl.BlockSpec((MOBA_BLOCK, LANES), lambda b, hp, i: (b * nq + i, hp)),
        out_shape=jax.ShapeDtypeStruct((t, ATTN_WIDTH), BF16),
        scratch_shapes=[
            pltpu.VMEM((nq, LANES), F32),
            pltpu.VMEM((HEADS_PER_LANE_TILE, nq, MOBA_BLOCK), F32),
        ],
        compiler_params=pltpu.CompilerParams(
            dimension_semantics=("arbitrary", "arbitrary", "arbitrary"), vmem_limit_bytes=VMEM_LIMIT),
        name="moba",
    )(slopes, proj, proj, vt3)


def _conv_silu(x, carry, w, b):
    y = w[3:4, :] * x + b
    for k in range(1, SSD_CONV):
        y = y + w[SSD_CONV - 1 - k:SSD_CONV - k, :] * _shift_rows(x, k, carry)
    return _silu(y)


def _ssd_kernel(z_ref, xs_ref, b_ref, c_ref, dt_ref, cw_ref, cb_ref, dtb_ref, alog_ref, dsk_ref, nw_ref,
                e_ref, y_ref, st_ref, cx_ref, cbc_ref):
    L = SSD_CHUNK

    @pl.when(pl.program_id(1) == 0)
    def _():
        st_ref[...] = jnp.zeros_like(st_ref)
        cx_ref[...] = jnp.zeros_like(cx_ref)
        cbc_ref[...] = jnp.zeros_like(cbc_ref)

    xs_raw = xs_ref[...].astype(F32)
    bc_raw = jnp.concatenate([b_ref[...], c_ref[...]], axis=1).astype(F32)
    xs = _conv_silu(xs_raw, cx_ref[...], cw_ref[:, :SSD_D_INNER], cb_ref[:, :SSD_D_INNER])
    bc = _conv_silu(bc_raw, cbc_ref[...], cw_ref[:, SSD_D_INNER:], cb_ref[:, SSD_D_INNER:])
    cx_ref[...] = xs_raw[L - 8:, :]
    cbc_ref[...] = bc_raw[L - 8:, :]
    bm = bc[:, :SSD_GN]
    cm = bc[:, SSD_GN:]

    xdt_in = dt_ref[...] + dtb_ref[...]
    dt = jnp.maximum(xdt_in, 0.0) + jnp.log1p(jnp.exp(-jnp.abs(xdt_in)))
    adt = dt * (-jnp.exp(alog_ref[...]))
    row = lax.broadcasted_iota(jnp.int32, (L, L), 0)
    col = lax.broadcasted_iota(jnp.int32, (L, L), 1)
    causal = row >= col
    tri = causal.astype(BF16)
    a_hi, a_mid, a_lo = _split3(adt)
    acs = _dot(tri, a_hi) + _dot(tri, a_mid) + _dot(tri, a_lo)
    acs_t = acs.T
    eacs = jnp.exp(acs)
    wend = jnp.exp(acs[L - 1:L, :] - acs)

    stacked = jnp.concatenate([dt, eacs, wend], axis=0)
    s_hi, s_mid, s_lo = _split3(stacked)
    e = e_ref[...]
    expanded = _dot(s_hi, e) + _dot(s_mid, e) + _dot(s_lo, e)
    dt_x = expanded[:L]
    eacs_x = expanded[L:2 * L]
    wend_x = expanded[2 * L:]

    xdt = xs * dt_x
    xdt_b = xdt.astype(BF16)
    xw_b = (xdt * wend_x).astype(BF16)
    lane = lax.broadcasted_iota(jnp.int32, (L, LANES), 1)
    first_head = lane < SSD_HEAD_DIM
    z = z_ref[...].astype(F32)

    for g in range(SSD_N_GROUPS):
        gl = slice(g * SSD_GROUP_W, (g + 1) * SSD_GROUP_W)
        bg = bm[:, g * SSD_D_STATE:(g + 1) * SSD_D_STATE]
        cg = cm[:, g * SSD_D_STATE:(g + 1) * SSD_D_STATE].astype(BF16)
        cb = _dot_nt(cg, bg.astype(BF16))
        y_inter = _dot(cg, st_ref[:, gl].astype(BF16))
        pairs = []
        for pr in range(SSD_GROUP_W // LANES):
            h0 = (g * SSD_GROUP_W + pr * LANES) // SSD_HEAD_DIM
            xp = xdt_b[:, g * SSD_GROUP_W + pr * LANES:g * SSD_GROUP_W + (pr + 1) * LANES]
            ys = []
            for h in (h0, h0 + 1):
                seg = acs[:, h:h + 1] - acs_t[h:h + 1, :]
                decay = jnp.exp(jnp.where(causal, seg, NEG_BIG))
                ys.append(_dot((cb * decay).astype(BF16), xp))
            pairs.append(jnp.where(first_head, ys[0], ys[1]))
        y = jnp.concatenate(pairs, axis=1) + y_inter * eacs_x[:, gl] + dsk_ref[:, gl] * xs[:, gl]
        st_ref[:, gl] = st_ref[:, gl] * eacs_x[L - 1:L, gl] + _dot(bg.T.astype(BF16), xw_b[:, gl])
        y = y * _silu(z[:, gl])
        y_ref[:, gl] = _rms(y, nw_ref[:, gl]).astype(y_ref.dtype)


def _ssd(proj, dt_raw, cw, cb, dtb, alog, dsk, nw, expand, batch, seq):
    t = proj.shape[0]
    nc = seq // SSD_CHUNK
    L = SSD_CHUNK
    conv_dim = SSD_D_INNER + 2 * SSD_GN

    def rows(b, c):
        return b * nc + c

    full = lambda shape: pl.BlockSpec(shape, lambda b, c: (0, 0))
    return pl.pallas_call(
        _ssd_kernel,
        grid=(batch, nc),
        in_specs=[
            pl.BlockSpec((L, SSD_D_INNER), lambda b, c: (rows(b, c), Z_OFF // SSD_D_INNER)),
            pl.BlockSpec((L, SSD_D_INNER), lambda b, c: (rows(b, c), XS_OFF // SSD_D_INNER)),
            pl.BlockSpec((L, SSD_GN), lambda b, c: (rows(b, c), B_OFF // SSD_GN)),
            pl.BlockSpec((L, SSD_GN), lambda b, c: (rows(b, c), C_OFF // SSD_GN)),
            pl.BlockSpec((L, LANES), lambda b, c: (rows(b, c), 0)),
            full((SSD_CONV, conv_dim)),
            full((1, conv_dim)),
            full((1, LANES)),
            full((1, LANES)),
            full((1, SSD_D_INNER)),
            full((1, SSD_D_INNER)),
            full((LANES, SSD_D_INNER)),
        ],
        out_specs=pl.BlockSpec((L, SSD_D_INNER), lambda b, c: (rows(b, c), 0)),
        out_shape=jax.ShapeDtypeStruct((t, SSD_D_INNER), BF16),
        scratch_shapes=[
            pltpu.VMEM((SSD_D_STATE, SSD_D_INNER), F32),
            pltpu.VMEM((8, SSD_D_INNER), F32),
            pltpu.VMEM((8, 2 * SSD_GN), F32),
        ],
        compiler_params=pltpu.CompilerParams(
            dimension_semantics=("arbitrary", "arbitrary"), vmem_limit_bytes=VMEM_LIMIT),
        name="ssd",
    )(proj, proj, proj, proj, dt_raw, cw, cb, dtb, alog, dsk, nw, expand)


def _mix_kernel(yn_ref, att_ref, gs_ref, ga_ref, x_ref, wssd_ref, wattn_ref, wout_ref, nw_ref, o_ref):
    y_ssd = _dot(yn_ref[...], wssd_ref[...])
    y_attn = _dot(att_ref[...], wattn_ref[...])
    mixed = _sigmoid(gs_ref[...].astype(F32)) * y_ssd + _sigmoid(ga_ref[...].astype(F32)) * y_attn
    o = _dot(mixed.astype(BF16), wout_ref[...])
    o_ref[...] = x_ref[...] + _rms(o, nw_ref[...])


def _mix(ynorm, att, proj, x2, wssd, wattn, wout, nw, tm=512):
    t = x2.shape[0]
    full = lambda shape: pl.BlockSpec(shape, lambda i: (0, 0))
    return pl.pallas_call(
        _mix_kernel,
        grid=(t // tm,),
        in_specs=[
            pl.BlockSpec((tm, SSD_D_INNER), lambda i: (i, 0)),
            pl.BlockSpec((tm, ATTN_WIDTH), lambda i: (i, 0)),
            pl.BlockSpec((tm, D_MODEL), lambda i: (i, GS_OFF // D_MODEL)),
            pl.BlockSpec((tm, D_MODEL), lambda i: (i, GA_OFF // D_MODEL)),
            pl.BlockSpec((tm, D_MODEL), lambda i: (i, 0)),
            full((SSD_D_INNER, D_MODEL)),
            full((ATTN_WIDTH, D_MODEL)),
            full((D_MODEL, D_MODEL)),
            full((1, D_MODEL)),
        ],
        out_specs=pl.BlockSpec((tm, D_MODEL), lambda i: (i, 0)),
        out_shape=jax.ShapeDtypeStruct((t, D_MODEL), F32),
        compiler_params=pltpu.CompilerParams(
            dimension_semantics=("arbitrary",), vmem_limit_bytes=VMEM_LIMIT),
        name="mix",
    )(ynorm, att, proj, proj, x2, wssd, wattn, wout, nw)


def _ffn_kernel(x_ref, nw1_ref, wg_ref, wu_ref, cw_ref, cb_ref, wd_ref, nw2_ref, o_ref,
                h_ref, acc_ref, carry_ref, *, tiles_per_seq):
    i = pl.program_id(0)
    k = pl.program_id(1)
    tm = x_ref.shape[0]

    @pl.when((i == 0) & (k == 0))
    def _():
        carry_ref[...] = jnp.zeros_like(carry_ref)

    @pl.when(k == 0)
    def _():
        h_ref[...] = _rms(x_ref[...], nw1_ref[...]).astype(BF16)
        acc_ref[...] = jnp.zeros_like(acc_ref)

    h = h_ref[...]
    gate = _dot(h, wg_ref[...])
    up = _dot(h, wu_ref[...])
    seq_start = (i % tiles_per_seq) == 0
    carry = jnp.where(seq_start, 0.0, carry_ref[k])
    carry_ref[k] = gate[tm - 8:, :]
    cw = cw_ref[...]
    conv = cw[FFN_CONV - 1:FFN_CONV, :] * gate + cb_ref[...]
    for s in range(1, FFN_CONV):
        conv = conv + cw[FFN_CONV - 1 - s:FFN_CONV - s, :] * _shift_rows(gate, s, carry)
    c0 = 0.7978845608028654
    gelu = 0.5 * conv * (1.0 + jnp.tanh(c0 * (conv + 0.044715 * (conv * conv * conv))))
    acc_ref[...] += _dot((gelu * up).astype(BF16), wd_ref[...])

    @pl.when(k == pl.num_programs(1) - 1)
    def _():
        o_ref[...] = x_ref[...] + _rms(acc_ref[...], nw2_ref[...])


def _ffn(x1, nw1, wup, cw, cb, wd, nw2, seq, tm=1024, tk=512):
    t = x1.shape[0]
    nk = FFN_HIDDEN // tk
    kern = functools.partial(_ffn_kernel, tiles_per_seq=seq // tm)
    return pl.pallas_call(
        kern,
        grid=(t // tm, nk),
        in_specs=[
            pl.BlockSpec((tm, D_MODEL), lambda i, k: (i, 0)),
            pl.BlockSpec((1, D_MODEL), lambda i, k: (0, 0)),
            pl.BlockSpec((D_MODEL, tk), lambda i, k: (0, k)),
            pl.BlockSpec((D_MODEL, tk), lambda i, k: (0, nk + k)),
            pl.BlockSpec((FFN_CONV, tk), lambda i, k: (0, k)),
            pl.BlockSpec((1, tk), lambda i, k: (0, k)),
            pl.BlockSpec((tk, D_MODEL), lambda i, k: (k, 0)),
            pl.BlockSpec((1, D_MODEL), lambda i, k: (0, 0)),
        ],
        out_specs=pl.BlockSpec((tm, D_MODEL), lambda i, k: (i, 0)),
        out_shape=jax.ShapeDtypeStruct((t, D_MODEL), F32),
        scratch_shapes=[
            pltpu.VMEM((tm, D_MODEL), BF16),
            pltpu.VMEM((tm, D_MODEL), F32),
            pltpu.VMEM((nk, 8, tk), F32),
        ],
        compiler_params=pltpu.CompilerParams(
            dimension_semantics=("arbitrary", "arbitrary"), vmem_limit_bytes=VMEM_LIMIT),
        name="ffn",
    )(x1, nw1, wup, wup, cw, cb, wd, nw2)


def _pad_lanes(v):
    return jnp.pad(v.astype(F32), (0, LANES - v.shape[0]))[None, :]


def kernel(x, pre_mix_norm, w_in, ssd_conv_w, ssd_conv_b, ssd_dt_bias, ssd_a_log, ssd_d_skip, ssd_out_norm,
           w_ssd_branch, w_attn_branch, w_out, post_mix_norm, pre_ffn_norm, w_ffn_up, ffn_conv_w, ffn_conv_b,
           w_ffn_down, post_ffn_norm):
    batch, seq, d = x.shape
    assert d == D_MODEL and seq % 1024 == 0
    depth = w_in.shape[0]
    x2 = x.reshape(batch * seq, d)

    slopes = 2.0 ** (-8.0 * jnp.arange(1, ATTN_N_HEADS + 1, dtype=F32) / ATTN_N_HEADS)
    head_of_channel = jnp.arange(SSD_D_INNER) // SSD_HEAD_DIM
    expand = (jnp.arange(LANES)[:, None] == head_of_channel[None, :]).astype(BF16)
    st = _IN_STARTS

    for li in range(depth):
        w = w_in[li]
        wp = jnp.concatenate([w[:, st[0]:st[2]], w[:, st[3]:st[5]], w[:, st[6]:]], axis=1).astype(BF16)
        wdt = jnp.pad(w[:, st[2]:st[3]], ((0, 0), (0, LANES - SSD_N_HEADS))).astype(BF16)
        wvt = w[:, st[5]:st[6]].T.astype(BF16)
        nw_mix = pre_mix_norm[li][None, :]

        proj, dt_raw = _inproj(x2, nw_mix, wp, wdt)
        vt3 = _vproj_t(x2, nw_mix, wvt)
        att = _moba(slopes, proj, vt3, batch, seq)
        ynorm = _ssd(proj, dt_raw, ssd_conv_w[li], ssd_conv_b[li][None, :], _pad_lanes(ssd_dt_bias[li]),
                     _pad_lanes(ssd_a_log[li]), jnp.repeat(ssd_d_skip[li].astype(F32), SSD_HEAD_DIM)[None, :],
                     ssd_out_norm[li][None, :], expand, batch, seq)
        x1 = _mix(ynorm, att, proj, x2, w_ssd_branch[li].astype(BF16), w_attn_branch[li].astype(BF16),
                  w_out[li].astype(BF16), post_mix_norm[li][None, :])
        x2 = _ffn(x1, pre_ffn_norm[li][None, :], w_ffn_up[li].astype(BF16), ffn_conv_w[li],
                  ffn_conv_b[li][None, :], w_ffn_down[li].astype(BF16), post_ffn_norm[li][None, :], seq)
    return x2.reshape(batch, seq, d)
```

```python
import functools

import jax
import jax.numpy as jnp
from jax import lax
from jax.experimental import pallas as pl
from jax.experimental.pallas import tpu as pltpu

F32 = jnp.float32
BF16 = jnp.bfloat16

D_MODEL = 1024
SSD_D_INNER = 2048
SSD_HEAD_DIM = 64
SSD_N_HEADS = 32
SSD_N_GROUPS = 4
SSD_D_STATE = 128
SSD_CONV = 4
SSD_CHUNK = 128
SSD_GN = SSD_N_GROUPS * SSD_D_STATE
SSD_GROUP_W = SSD_D_INNER // SSD_N_GROUPS
ATTN_HEAD_DIM = 64
ATTN_N_HEADS = 16
ATTN_WIDTH = 1024
MOBA_BLOCK = 256
MOBA_TOPK = 3
FFN_HIDDEN = 4096
FFN_CONV = 3
NORM_EPS = 1e-6

LANES = 128
HEADS_PER_LANE_TILE = LANES // ATTN_HEAD_DIM
NEG_BIG = -1e30

_IN_SIZES = (SSD_D_INNER, SSD_D_INNER + 2 * SSD_GN, SSD_N_HEADS, ATTN_WIDTH, ATTN_WIDTH, ATTN_WIDTH,
             D_MODEL, D_MODEL)
_IN_STARTS = tuple(sum(_IN_SIZES[:j]) for j in range(len(_IN_SIZES)))
Z_OFF = 0
XS_OFF = 2048
B_OFF = 4096
C_OFF = 4608
Q_OFF = 5120
K_OFF = 6144
GS_OFF = 7168
GA_OFF = 8192
PROJ_COLS = 9216

VMEM_LIMIT = 48 * 1024 * 1024


def _dot(a, b):
    return jnp.dot(a, b, preferred_element_type=F32)


def _dot_nt(a, b):
    return lax.dot_general(a, b, (((1,), (1,)), ((), ())), preferred_element_type=F32)


def _rms(x, w):
    return x * lax.rsqrt(jnp.mean(x * x, axis=-1, keepdims=True) + NORM_EPS) * w


def _sigmoid(x):
    return 1.0 / (1.0 + jnp.exp(-x))


def _silu(x):
    return x * _sigmoid(x)


def _split3(v):
    hi = v.astype(BF16)
    r1 = v - hi.astype(F32)
    mid = r1.astype(BF16)
    lo = (r1 - mid.astype(F32)).astype(BF16)
    return hi, mid, lo


def _shift_rows(x, k, carry):
    r = pltpu.roll(x, k, axis=0)
    row = lax.broadcasted_iota(jnp.int32, x.shape, 0)
    hist = carry.shape[0]
    for t in range(k):
        r = jnp.where(row == t, carry[hist - k + t:hist - k + t + 1, :], r)
    return r


def _inproj_kernel(x_ref, nw_ref, w_ref, wdt_ref, proj_ref, dt_ref, h_ref):
    @pl.when(pl.program_id(1) == 0)
    def _():
        h = _rms(x_ref[...], nw_ref[...]).astype(BF16)
        h_ref[...] = h
        dt_ref[...] = _dot(h, wdt_ref[...])

    proj_ref[...] = _dot(h_ref[...], w_ref[...]).astype(BF16)


def _inproj(x2, nw, wp, wdt, tm=1024, tn=1024):
    t = x2.shape[0]
    return pl.pallas_call(
        _inproj_kernel,
        grid=(t // tm, PROJ_COLS // tn),
        in_specs=[
            pl.BlockSpec((tm, D_MODEL), lambda i, j: (i, 0)),
            pl.BlockSpec((1, D_MODEL), lambda i, j: (0, 0)),
            pl.BlockSpec((D_MODEL, tn), lambda i, j: (0, j)),
            pl.BlockSpec((D_MODEL, LANES), lambda i, j: (0, 0)),
        ],
        out_specs=[
            pl.BlockSpec((tm, tn), lambda i, j: (i, j)),
            pl.BlockSpec((tm, LANES), lambda i, j: (i, 0)),
        ],
        out_shape=[
            jax.ShapeDtypeStruct((t, PROJ_COLS), BF16),
            jax.ShapeDtypeStruct((t, LANES), F32),
        ],
        scratch_shapes=[pltpu.VMEM((tm, D_MODEL), BF16)],
        compiler_params=pltpu.CompilerParams(
            dimension_semantics=("arbitrary", "arbitrary"), vmem_limit_bytes=VMEM_LIMIT),
        name="inproj",
    )(x2, nw, wp, wdt)


def _vt_kernel(x_ref, nw_ref, wvt_ref, vt_ref):
    h = _rms(x_ref[...], nw_ref[...]).astype(BF16)
    vt = _dot_nt(wvt_ref[...], h)
    for c in range(vt_ref.shape[0]):
        vt_ref[c] = vt[:, c * MOBA_BLOCK:(c + 1) * MOBA_BLOCK].astype(BF16)


def _vproj_t(x2, nw, wvt, tm=512):
    t = x2.shape[0]
    return pl.pallas_call(
        _vt_kernel,
        grid=(t // tm,),
        in_specs=[
            pl.BlockSpec((tm, D_MODEL), lambda i: (i, 0)),
            pl.BlockSpec((1, D_MODEL), lambda i: (0, 0)),
            pl.BlockSpec((ATTN_WIDTH, D_MODEL), lambda i: (0, 0)),
        ],
        out_specs=pl.BlockSpec((tm // MOBA_BLOCK, ATTN_WIDTH, MOBA_BLOCK), lambda i: (i, 0, 0)),
        out_shape=jax.ShapeDtypeStruct((t // MOBA_BLOCK, ATTN_WIDTH, MOBA_BLOCK), BF16),
        compiler_params=pltpu.CompilerParams(
            dimension_semantics=("arbitrary",), vmem_limit_bytes=VMEM_LIMIT),
        name="vproj_t",
    )(x2, nw, wvt)


def _moba_kernel(slopes_ref, q_ref, k_ref, vt_ref, o_ref, km_ref, sb_ref, sc0_ref, sc1_ref):
    hp = pl.program_id(1)
    i = pl.program_id(2)
    nb = km_ref.shape[0]
    blk = MOBA_BLOCK
    heads = range(HEADS_PER_LANE_TILE)

    @pl.when(i == 0)
    def _():
        km_ref[...] = jnp.zeros_like(km_ref)

    qs = q_ref[...] * ATTN_HEAD_DIM ** -0.5
    lane = lax.broadcasted_iota(jnp.int32, (1, LANES), 1)
    key_pos = lax.broadcasted_iota(jnp.int32, (blk, blk), 0)
    qry_pos = lax.broadcasted_iota(jnp.int32, (blk, blk), 1)
    blk_id = lax.broadcasted_iota(jnp.int32, (nb, blk), 0)
    k_own = k_ref[pl.ds(pl.multiple_of(i * blk, blk), blk), :]
    vt_own = vt_ref[i]
    km_parts = _split3(km_ref[...])
    valid = blk_id < i

    qm, vrows, slope, col_bias = [], [], [], []
    for hh in heads:
        lo_lane = hh * ATTN_HEAD_DIM
        qm.append(jnp.where((lane >= lo_lane) & (lane < lo_lane + ATTN_HEAD_DIM), qs, jnp.zeros_like(qs)))
        vrows.append(slice(lo_lane, lo_lane + ATTN_HEAD_DIM))
        slope.append(slopes_ref[hp * HEADS_PER_LANE_TILE + hh])
        col_bias.append(slope[hh] * key_pos.astype(F32))

    for hh in heads:
        gate = sum(_dot_nt(part, qm[hh]) for part in km_parts)
        g = jnp.where(valid, gate, -jnp.inf)
        sel = jnp.zeros((nb, blk), jnp.bool_)
        for _ in range(MOBA_TOPK):
            top = jnp.max(g, axis=0, keepdims=True)
            first = jnp.min(jnp.where(g == top, blk_id, nb), axis=0, keepdims=True)
            pick = blk_id == first
            sel = sel | (pick & (top > -jnp.inf))
            g = jnp.where(pick, -jnp.inf, g)
        sb_ref[hh] = jnp.where(sel, 0.0, NEG_BIG)

    def store_scores(j, dst_ref):
        j = jnp.minimum(j, nb - 1)
        kj = k_ref[pl.ds(pl.multiple_of(j * blk, blk), blk), :]
        for hh in heads:
            dst_ref[hh] = _dot_nt(kj, qm[hh])

    def softmax_step(src_ref, j, carry):
        vtj = vt_ref[j]
        probs = []
        for hh in heads:
            m, l, _ = carry[3 * hh:3 * hh + 3]
            row_bias = sb_ref[hh, pl.ds(j, 1), :] + slope[hh] * (blk * (j - i)).astype(F32)
            s = src_ref[hh] + col_bias[hh] + row_bias
            m_new = jnp.maximum(m, jnp.max(s, axis=0, keepdims=True))
            alpha = jnp.exp(m - m_new)
            p = jnp.exp(s - m_new)
            l = alpha * l + jnp.sum(p, axis=0, keepdims=True)
            probs.append((m_new, l, alpha, p.astype(BF16)))
        out = []
        for hh in heads:
            m_new, l, alpha, p = probs[hh]
            out += [m_new, l, alpha * carry[3 * hh + 2] + _dot(vtj[vrows[hh], :], p)]
        return tuple(out)

    store_scores(0, sc0_ref)

    state = []
    for hh in heads:
        s = _dot_nt(k_own, qm[hh]) + col_bias[hh]
        s = jnp.where(key_pos <= qry_pos, s, NEG_BIG)
        m = jnp.max(s, axis=0, keepdims=True)
        p = jnp.exp(s - m)
        l = jnp.sum(p, axis=0, keepdims=True)
        acc = _dot(vt_own[vrows[hh], :], p.astype(BF16))
        state += [m, l, acc]

    def body(t, carry):
        j0 = 2 * t
        store_scores(j0 + 1, sc1_ref)
        carry = softmax_step(sc0_ref, j0, carry)
        store_scores(j0 + 2, sc0_ref)
        return softmax_step(sc1_ref, j0 + 1, carry)

    state = lax.fori_loop(0, (i + 1) // 2, body, tuple(state))
    outs = [state[3 * hh + 2] / state[3 * hh + 1] for hh in heads]
    o_ref[...] = jnp.concatenate(outs, axis=0).T.astype(o_ref.dtype)
    km_ref[pl.ds(i, 1), :] = jnp.mean(k_own.astype(F32), axis=0, keepdims=True)


def _moba(slopes, proj, vt3, batch, seq):
    t = proj.shape[0]
    nq = seq // MOBA_BLOCK
    n_pairs = ATTN_N_HEADS // HEADS_PER_LANE_TILE
    q_blk0 = Q_OFF // LANES
    k_blk0 = K_OFF // LANES
    return pl.pallas_call(
        _moba_kernel,
        grid=(batch, n_pairs, nq),
        in_specs=[
            pl.BlockSpec(memory_space=pltpu.SMEM),
            pl.BlockSpec((MOBA_BLOCK, LANES), lambda b, hp, i: (b * nq + i, q_blk0 + hp)),
            pl.BlockSpec((seq, LANES), lambda b, hp, i: (b, k_blk0 + hp)),
            pl.BlockSpec((nq, LANES, MOBA_BLOCK), lambda b, hp, i: (b, hp, 0)),
        ],
        out_specs=pl.BlockSpec((MOBA_BLOCK, LANES), lambda b, hp, i: (b * nq + i, hp)),
        out_shape=jax.ShapeDtypeStruct((t, ATTN_WIDTH), BF16),
        scratch_shapes=[
            pltpu.VMEM((nq, LANES), F32),
            pltpu.VMEM((HEADS_PER_LANE_TILE, nq, MOBA_BLOCK), F32),
            pltpu.VMEM((HEADS_PER_LANE_TILE, MOBA_BLOCK, MOBA_BLOCK), F32),
            pltpu.VMEM((HEADS_PER_LANE_TILE, MOBA_BLOCK, MOBA_BLOCK), F32),
        ],
        compiler_params=pltpu.CompilerParams(
            dimension_semantics=("arbitrary", "arbitrary", "arbitrary"), vmem_limit_bytes=VMEM_LIMIT),
        name="moba",
    )(slopes, proj, proj, vt3)


def _conv_silu(x, carry, w, b):
    y = w[3:4, :] * x + b
    for k in range(1, SSD_CONV):
        y = y + w[SSD_CONV - 1 - k:SSD_CONV - k, :] * _shift_rows(x, k, carry)
    return _silu(y)


def _ssd_kernel(z_ref, xs_ref, b_ref, c_ref, dt_ref, cw_ref, cb_ref, dtb_ref, alog_ref, dsk_ref, nw_ref,
                e_ref, y_ref, st_ref, cx_ref, cbc_ref):
    L = SSD_CHUNK

    @pl.when(pl.program_id(1) == 0)
    def _():
        st_ref[...] = jnp.zeros_like(st_ref)
        cx_ref[...] = jnp.zeros_like(cx_ref)
        cbc_ref[...] = jnp.zeros_like(cbc_ref)

    xs_raw = xs_ref[...].astype(F32)
    bc_raw = jnp.concatenate([b_ref[...], c_ref[...]], axis=1).astype(F32)
    xs = _conv_silu(xs_raw, cx_ref[...], cw_ref[:, :SSD_D_INNER], cb_ref[:, :SSD_D_INNER])
    bc = _conv_silu(bc_raw, cbc_ref[...], cw_ref[:, SSD_D_INNER:], cb_ref[:, SSD_D_INNER:])
    cx_ref[...] = xs_raw[L - 8:, :]
    cbc_ref[...] = bc_raw[L - 8:, :]
    bm = bc[:, :SSD_GN]
    cm = bc[:, SSD_GN:]

    xdt_in = dt_ref[...] + dtb_ref[...]
    dt = jnp.maximum(xdt_in, 0.0) + jnp.log1p(jnp.exp(-jnp.abs(xdt_in)))
    adt = dt * (-jnp.exp(alog_ref[...]))
    row = lax.broadcasted_iota(jnp.int32, (L, L), 0)
    col = lax.broadcasted_iota(jnp.int32, (L, L), 1)
    causal = row >= col
    tri = causal.astype(BF16)
    a_hi, a_mid, a_lo = _split3(adt)
    acs = _dot(tri, a_hi) + _dot(tri, a_mid) + _dot(tri, a_lo)
    acs_t = acs.T
    eacs = jnp.exp(acs)
    wend = jnp.exp(acs[L - 1:L, :] - acs)

    stacked = jnp.concatenate([dt, eacs, wend], axis=0)
    s_hi, s_mid, s_lo = _split3(stacked)
    e = e_ref[...]
    expanded = _dot(s_hi, e) + _dot(s_mid, e) + _dot(s_lo, e)
    dt_x = expanded[:L]
    eacs_x = expanded[L:2 * L]
    wend_x = expanded[2 * L:]

    xdt = xs * dt_x
    xdt_b = xdt.astype(BF16)
    xw_b = (xdt * wend_x).astype(BF16)
    lane = lax.broadcasted_iota(jnp.int32, (L, LANES), 1)
    first_head = lane < SSD_HEAD_DIM
    z = z_ref[...].astype(F32)

    for g in range(SSD_N_GROUPS):
        gl = slice(g * SSD_GROUP_W, (g + 1) * SSD_GROUP_W)
        bg = bm[:, g * SSD_D_STATE:(g + 1) * SSD_D_STATE]
        cg = cm[:, g * SSD_D_STATE:(g + 1) * SSD_D_STATE].astype(BF16)
        cb = _dot_nt(cg, bg.astype(BF16))
        y_inter = _dot(cg, st_ref[:, gl].astype(BF16))
        pairs = []
        for pr in range(SSD_GROUP_W // LANES):
            h0 = (g * SSD_GROUP_W + pr * LANES) // SSD_HEAD_DIM
            xp = xdt_b[:, g * SSD_GROUP_W + pr * LANES:g * SSD_GROUP_W + (pr + 1) * LANES]
            ys = []
            for h in (h0, h0 + 1):
                seg = acs[:, h:h + 1] - acs_t[h:h + 1, :]
                decay = jnp.exp(jnp.where(causal, seg, NEG_BIG))
                ys.append(_dot((cb * decay).astype(BF16), xp))
            pairs.append(jnp.where(first_head, ys[0], ys[1]))
        y = jnp.concatenate(pairs, axis=1) + y_inter * eacs_x[:, gl] + dsk_ref[:, gl] * xs[:, gl]
        st_ref[:, gl] = st_ref[:, gl] * eacs_x[L - 1:L, gl] + _dot(bg.T.astype(BF16), xw_b[:, gl])
        y = y * _silu(z[:, gl])
        y_ref[:, gl] = _rms(y, nw_ref[:, gl]).astype(y_ref.dtype)


def _ssd(proj, dt_raw, cw, cb, dtb, alog, dsk, nw, expand, batch, seq):
    t = proj.shape[0]
    nc = seq // SSD_CHUNK
    L = SSD_CHUNK
    conv_dim = SSD_D_INNER + 2 * SSD_GN

    def rows(b, c):
        return b * nc + c

    full = lambda shape: pl.BlockSpec(shape, lambda b, c: (0, 0))
    return pl.pallas_call(
        _ssd_kernel,
        grid=(batch, nc),
        in_specs=[
            pl.BlockSpec((L, SSD_D_INNER), lambda b, c: (rows(b, c), Z_OFF // SSD_D_INNER)),
            pl.BlockSpec((L, SSD_D_INNER), lambda b, c: (rows(b, c), XS_OFF // SSD_D_INNER)),
            pl.BlockSpec((L, SSD_GN), lambda b, c: (rows(b, c), B_OFF // SSD_GN)),
            pl.BlockSpec((L, SSD_GN), lambda b, c: (rows(b, c), C_OFF // SSD_GN)),
            pl.BlockSpec((L, LANES), lambda b, c: (rows(b, c), 0)),
            full((SSD_CONV, conv_dim)),
            full((1, conv_dim)),
            full((1, LANES)),
            full((1, LANES)),
            full((1, SSD_D_INNER)),
            full((1, SSD_D_INNER)),
            full((LANES, SSD_D_INNER)),
        ],
        out_specs=pl.BlockSpec((L, SSD_D_INNER), lambda b, c: (rows(b, c), 0)),
        out_shape=jax.ShapeDtypeStruct((t, SSD_D_INNER), BF16),
        scratch_shapes=[
            pltpu.VMEM((SSD_D_STATE, SSD_D_INNER), F32),
            pltpu.VMEM((8, SSD_D_INNER), F32),
            pltpu.VMEM((8, 2 * SSD_GN), F32),
        ],
        compiler_params=pltpu.CompilerParams(
            dimension_semantics=("arbitrary", "arbitrary"), vmem_limit_bytes=VMEM_LIMIT),
        name="ssd",
    )(proj, proj, proj, proj, dt_raw, cw, cb, dtb, alog, dsk, nw, expand)


def _mix_kernel(yn_ref, att_ref, gs_ref, ga_ref, x_ref, wssd_ref, wattn_ref, wout_ref, nw_ref, o_ref):
    y_ssd = _dot(yn_ref[...], wssd_ref[...])
    y_attn = _dot(att_ref[...], wattn_ref[...])
    mixed = _sigmoid(gs_ref[...].astype(F32)) * y_ssd + _sigmoid(ga_ref[...].astype(F32)) * y_attn
    o = _dot(mixed.astype(BF16), wout_ref[...])
    o_ref[...] = x_ref[...] + _rms(o, nw_ref[...])


def _mix(ynorm, att, proj, x2, wssd, wattn, wout, nw, tm=512):
    t = x2.shape[0]
    full = lambda shape: pl.BlockSpec(shape, lambda i: (0, 0))
    return pl.pallas_call(
        _mix_kernel,
        grid=(t // tm,),
        in_specs=[
            pl.BlockSpec((tm, SSD_D_INNER), lambda i: (i, 0)),
            pl.BlockSpec((tm, ATTN_WIDTH), lambda i: (i, 0)),
            pl.BlockSpec((tm, D_MODEL), lambda i: (i, GS_OFF // D_MODEL)),
            pl.BlockSpec((tm, D_MODEL), lambda i: (i, GA_OFF // D_MODEL)),
            pl.BlockSpec((tm, D_MODEL), lambda i: (i, 0)),
            full((SSD_D_INNER, D_MODEL)),
            full((ATTN_WIDTH, D_MODEL)),
            full((D_MODEL, D_MODEL)),
            full((1, D_MODEL)),
        ],
        out_specs=pl.BlockSpec((tm, D_MODEL), lambda i: (i, 0)),
        out_shape=jax.ShapeDtypeStruct((t, D_MODEL), F32),
        compiler_params=pltpu.CompilerParams(
            dimension_semantics=("arbitrary",), vmem_limit_bytes=VMEM_LIMIT),
        name="mix",
    )(ynorm, att, proj, proj, x2, wssd, wattn, wout, nw)


def _ffn_kernel(x_ref, nw1_ref, wg_ref, wu_ref, cw_ref, cb_ref, wd_ref, nw2_ref, o_ref,
                h_ref, acc_ref, carry_ref, *, tiles_per_seq):
    i = pl.program_id(0)
    k = pl.program_id(1)
    tm = x_ref.shape[0]

    @pl.when((i == 0) & (k == 0))
    def _():
        carry_ref[...] = jnp.zeros_like(carry_ref)

    @pl.when(k == 0)
    def _():
        h_ref[...] = _rms(x_ref[...], nw1_ref[...]).astype(BF16)
        acc_ref[...] = jnp.zeros_like(acc_ref)

    h = h_ref[...]
    gate = _dot(h, wg_ref[...])
    up = _dot(h, wu_ref[...])
    seq_start = (i % tiles_per_seq) == 0
    carry = jnp.where(seq_start, 0.0, carry_ref[k])
    carry_ref[k] = gate[tm - 8:, :]
    cw = cw_ref[...]
    conv = cw[FFN_CONV - 1:FFN_CONV, :] * gate + cb_ref[...]
    for s in range(1, FFN_CONV):
        conv = conv + cw[FFN_CONV - 1 - s:FFN_CONV - s, :] * _shift_rows(gate, s, carry)
    c0 = 0.7978845608028654
    gelu = 0.5 * conv * (1.0 + jnp.tanh(c0 * (conv + 0.044715 * (conv * conv * conv))))
    acc_ref[...] += _dot((gelu * up).astype(BF16), wd_ref[...])

    @pl.when(k == pl.num_programs(1) - 1)
    def _():
        o_ref[...] = x_ref[...] + _rms(acc_ref[...], nw2_ref[...])


def _ffn(x1, nw1, wup, cw, cb, wd, nw2, seq, tm=1024, tk=512):
    t = x1.shape[0]
    nk = FFN_HIDDEN // tk
    kern = functools.partial(_ffn_kernel, tiles_per_seq=seq // tm)
    return pl.pallas_call(
        kern,
        grid=(t // tm, nk),
        in_specs=[
            pl.BlockSpec((tm, D_MODEL), lambda i, k: (i, 0)),
            pl.BlockSpec((1, D_MODEL), lambda i, k: (0, 0)),
            pl.BlockSpec((D_MODEL, tk), lambda i, k: (0, k)),
            pl.BlockSpec((D_MODEL, tk), lambda i, k: (0, nk + k)),
            pl.BlockSpec((FFN_CONV, tk), lambda i, k: (0, k)),
            pl.BlockSpec((1, tk), lambda i, k: (0, k)),
            pl.BlockSpec((tk, D_MODEL), lambda i, k: (k, 0)),
            pl.BlockSpec((1, D_MODEL), lambda i, k: (0, 0)),
        ],
        out_specs=pl.BlockSpec((tm, D_MODEL), lambda i, k: (i, 0)),
        out_shape=jax.ShapeDtypeStruct((t, D_MODEL), F32),
        scratch_shapes=[
            pltpu.VMEM((tm, D_MODEL), BF16),
            pltpu.VMEM((tm, D_MODEL), F32),
            pltpu.VMEM((nk, 8, tk), F32),
        ],
        compiler_params=pltpu.CompilerParams(
            dimension_semantics=("arbitrary", "arbitrary"), vmem_limit_bytes=VMEM_LIMIT),
        name="ffn",
    )(x1, nw1, wup, wup, cw, cb, wd, nw2)


def _pad_lanes(v):
    return jnp.pad(v.astype(F32), (0, LANES - v.shape[0]))[None, :]


def kernel(x, pre_mix_norm, w_in, ssd_conv_w, ssd_conv_b, ssd_dt_bias, ssd_a_log, ssd_d_skip, ssd_out_norm,
           w_ssd_branch, w_attn_branch, w_out, post_mix_norm, pre_ffn_norm, w_ffn_up, ffn_conv_w, ffn_conv_b,
           w_ffn_down, post_ffn_norm):
    batch, seq, d = x.shape
    assert d == D_MODEL and seq % 1024 == 0
    depth = w_in.shape[0]
    x2 = x.reshape(batch * seq, d)

    slopes = 2.0 ** (-8.0 * jnp.arange(1, ATTN_N_HEADS + 1, dtype=F32) / ATTN_N_HEADS)
    head_of_channel = jnp.arange(SSD_D_INNER) // SSD_HEAD_DIM
    expand = (jnp.arange(LANES)[:, None] == head_of_channel[None, :]).astype(BF16)
    st = _IN_STARTS

    for li in range(depth):
        w = w_in[li]
        wp = jnp.concatenate([w[:, st[0]:st[2]], w[:, st[3]:st[5]], w[:, st[6]:]], axis=1).astype(BF16)
        wdt = jnp.pad(w[:, st[2]:st[3]], ((0, 0), (0, LANES - SSD_N_HEADS))).astype(BF16)
        wvt = w[:, st[5]:st[6]].T.astype(BF16)
        nw_mix = pre_mix_norm[li][None, :]

        proj, dt_raw = _inproj(x2, nw_mix, wp, wdt)
        vt3 = _vproj_t(x2, nw_mix, wvt)
        att = _moba(slopes, proj, vt3, batch, seq)
        ynorm = _ssd(proj, dt_raw, ssd_conv_w[li], ssd_conv_b[li][None, :], _pad_lanes(ssd_dt_bias[li]),
                     _pad_lanes(ssd_a_log[li]), jnp.repeat(ssd_d_skip[li].astype(F32), SSD_HEAD_DIM)[None, :],
                     ssd_out_norm[li][None, :], expand, batch, seq)
        x1 = _mix(ynorm, att, proj, x2, w_ssd_branch[li].astype(BF16), w_attn_branch[li].astype(BF16),
                  w_out[li].astype(BF16), post_mix_norm[li][None, :])
        x2 = _ffn(x1, pre_ffn_norm[li][None, :], w_ffn_up[li].astype(BF16), ffn_conv_w[li],
                  ffn_conv_b[li][None, :], w_ffn_down[li].astype(BF16), post_ffn_norm[li][None, :], seq)
    return x2.reshape(batch, seq, d)
```

```python
import functools

import jax
import jax.numpy as jnp
from jax import lax
from jax.experimental import pallas as pl
from jax.experimental.pallas import tpu as pltpu

F32 = jnp.float32
BF16 = jnp.bfloat16

D_MODEL = 1024
SSD_D_INNER = 2048
SSD_HEAD_DIM = 64
SSD_N_HEADS = 32
SSD_N_GROUPS = 4
SSD_D_STATE = 128
SSD_CONV = 4
SSD_CHUNK = 128
SSD_GN = SSD_N_GROUPS * SSD_D_STATE
SSD_GROUP_W = SSD_D_INNER // SSD_N_GROUPS
ATTN_HEAD_DIM = 64
ATTN_N_HEADS = 16
ATTN_WIDTH = 1024
MOBA_BLOCK = 256
MOBA_TOPK = 3
FFN_HIDDEN = 4096
FFN_CONV = 3
NORM_EPS = 1e-6

LANES = 128
HEADS_PER_LANE_TILE = LANES // ATTN_HEAD_DIM
NEG_BIG = -1e30
LOG2E = 1.4426950408889634

_IN_SIZES = (SSD_D_INNER, SSD_D_INNER + 2 * SSD_GN, SSD_N_HEADS, ATTN_WIDTH, ATTN_WIDTH, ATTN_WIDTH,
             D_MODEL, D_MODEL)
_IN_STARTS = tuple(sum(_IN_SIZES[:j]) for j in range(len(_IN_SIZES)))
Z_OFF = 0
XS_OFF = 2048
B_OFF = 4096
C_OFF = 4608
Q_OFF = 5120
K_OFF = 6144
GS_OFF = 7168
GA_OFF = 8192
PROJ_COLS = 9216

VMEM_LIMIT = 48 * 1024 * 1024


def _dot(a, b):
    return jnp.dot(a, b, preferred_element_type=F32)


def _dot_nt(a, b):
    return lax.dot_general(a, b, (((1,), (1,)), ((), ())), preferred_element_type=F32)


def _rms(x, w):
    return x * lax.rsqrt(jnp.mean(x * x, axis=-1, keepdims=True) + NORM_EPS) * w


def _sigmoid(x):
    return 1.0 / (1.0 + jnp.exp(-x))


def _silu(x):
    return x * _sigmoid(x)


def _split3(v):
    hi = v.astype(BF16)
    r1 = v - hi.astype(F32)
    mid = r1.astype(BF16)
    lo = (r1 - mid.astype(F32)).astype(BF16)
    return hi, mid, lo


def _shift_rows(x, k, carry):
    r = pltpu.roll(x, k, axis=0)
    row = lax.broadcasted_iota(jnp.int32, x.shape, 0)
    hist = carry.shape[0]
    for t in range(k):
        r = jnp.where(row == t, carry[hist - k + t:hist - k + t + 1, :], r)
    return r


def _inproj_kernel(x_ref, nw_ref, w_ref, wdt_ref, proj_ref, dt_ref, h_ref):
    @pl.when(pl.program_id(1) == 0)
    def _():
        h = _rms(x_ref[...], nw_ref[...]).astype(BF16)
        h_ref[...] = h
        dt_ref[...] = _dot(h, wdt_ref[...])

    proj_ref[...] = _dot(h_ref[...], w_ref[...]).astype(BF16)


def _inproj(x2, nw, wp, wdt, tm=1024, tn=1024):
    t = x2.shape[0]
    return pl.pallas_call(
        _inproj_kernel,
        grid=(t // tm, PROJ_COLS // tn),
        in_specs=[
            pl.BlockSpec((tm, D_MODEL), lambda i, j: (i, 0)),
            pl.BlockSpec((1, D_MODEL), lambda i, j: (0, 0)),
            pl.BlockSpec((D_MODEL, tn), lambda i, j: (0, j)),
            pl.BlockSpec((D_MODEL, LANES), lambda i, j: (0, 0)),
        ],
        out_specs=[
            pl.BlockSpec((tm, tn), lambda i, j: (i, j)),
            pl.BlockSpec((tm, LANES), lambda i, j: (i, 0)),
        ],
        out_shape=[
            jax.ShapeDtypeStruct((t, PROJ_COLS), BF16),
            jax.ShapeDtypeStruct((t, LANES), F32),
        ],
        scratch_shapes=[pltpu.VMEM((tm, D_MODEL), BF16)],
        compiler_params=pltpu.CompilerParams(
            dimension_semantics=("arbitrary", "arbitrary"), vmem_limit_bytes=VMEM_LIMIT),
        name="inproj",
    )(x2, nw, wp, wdt)


def _vt_kernel(x_ref, nw_ref, wvt_ref, vt_ref):
    h = _rms(x_ref[...], nw_ref[...]).astype(BF16)
    vt = _dot_nt(wvt_ref[...], h)
    for c in range(vt_ref.shape[0]):
        vt_ref[c] = vt[:, c * MOBA_BLOCK:(c + 1) * MOBA_BLOCK].astype(BF16)


def _vproj_t(x2, nw, wvt, tm=512):
    t = x2.shape[0]
    return pl.pallas_call(
        _vt_kernel,
        grid=(t // tm,),
        in_specs=[
            pl.BlockSpec((tm, D_MODEL), lambda i: (i, 0)),
            pl.BlockSpec((1, D_MODEL), lambda i: (0, 0)),
            pl.BlockSpec((ATTN_WIDTH, D_MODEL), lambda i: (0, 0)),
        ],
        out_specs=pl.BlockSpec((tm // MOBA_BLOCK, ATTN_WIDTH, MOBA_BLOCK), lambda i: (i, 0, 0)),
        out_shape=jax.ShapeDtypeStruct((t // MOBA_BLOCK, ATTN_WIDTH, MOBA_BLOCK), BF16),
        compiler_params=pltpu.CompilerParams(
            dimension_semantics=("arbitrary",), vmem_limit_bytes=VMEM_LIMIT),
        name="vproj_t",
    )(x2, nw, wvt)


def _moba_kernel(slopes_ref, q_ref, k_ref, vt_ref, o_ref, km_ref, kh_ref, vts_ref, sel_ref,
                 sc0_ref, sc1_ref, sc2_ref, pb0_ref, pb1_ref, pb2_ref):
    hp = pl.program_id(1)
    i = pl.program_id(2)
    nb = km_ref.shape[0]
    blk = MOBA_BLOCK
    heads = range(HEADS_PER_LANE_TILE)
    den_rows = 16
    vt_rows = ATTN_HEAD_DIM + den_rows

    @pl.when(i == 0)
    def _():
        km_ref[...] = jnp.zeros_like(km_ref)

    q_raw = q_ref[...]
    qs = (q_raw.astype(F32) * (ATTN_HEAD_DIM ** -0.5 * LOG2E)).astype(BF16)
    lane =lax.broadcasted_iota(jnp.int32, (1, LANES), 1)
    key_pos = lax.broadcasted_iota(jnp.int32, (blk, blk), 0)
    qry_pos = lax.broadcasted_iota(jnp.int32, (blk, blk), 1)
    key_lane = lax.broadcasted_iota(jnp.int32, (1, blk), 1)
    blk_id = lax.broadcasted_iota(jnp.int32, (nb, blk), 0)
    k_own = k_ref[...]
    vt_own = vt_ref[0]
    valid = blk_id < i

    head_lanes, vrows, slope = [], [], []
    for hh in heads:
        lo_lane = hh * ATTN_HEAD_DIM
        head_lanes.append((lane >= lo_lane) & (lane < lo_lane + ATTN_HEAD_DIM))
        vrows.append(slice(lo_lane, lo_lane + ATTN_HEAD_DIM))
        slope.append(slopes_ref[hp * HEADS_PER_LANE_TILE + hh] * LOG2E)

    kh_own = jnp.concatenate([jnp.where(head_lanes[hh], k_own, jnp.zeros_like(k_own)) for hh in heads], axis=0)
    kh_ref[i] = kh_own

    km = km_ref[...]
    gate_lhs = jnp.concatenate(
        [part for hh in heads for part in _split3(jnp.where(head_lanes[hh], km, 0.0))], axis=0)
    gate_all = _dot_nt(gate_lhs, q_raw)
    for hh in heads:
        gate = sum(gate_all[(3 * hh + c) * nb:(3 * hh + c + 1) * nb] for c in range(3))
        g = jnp.where(valid, gate, -jnp.inf)
        sel = jnp.zeros((nb, blk), jnp.bool_)
        for _ in range(MOBA_TOPK):
            top = jnp.max(g, axis=0, keepdims=True)
            first = jnp.min(jnp.where(g == top, blk_id, nb), axis=0, keepdims=True)
            pick = blk_id == first
            sel = sel | (pick & (top > -jnp.inf))
            g = jnp.where(pick, -jnp.inf, g)
        sel_ref[hh, :nb, :] = sel.astype(F32)
        sel_ref[hh, nb:, :] = jnp.zeros((sel_ref.shape[1] - nb, blk), F32)

    def store_scores(j, dst_ref):
        dst_ref[...] = _dot_nt(kh_ref[jnp.minimum(j, i)], qs)

    row0 = lax.broadcasted_iota(jnp.int32, (den_rows, blk), 0) == 0
    own_slot = nb
    for hh in heads:
        w = jnp.exp2(slope[hh] * (key_lane - (blk - 1)).astype(F32))
        v_rows = slice(hh * vt_rows, hh * vt_rows + ATTN_HEAD_DIM)
        d_rows = slice(hh * vt_rows + ATTN_HEAD_DIM, (hh + 1) * vt_rows)
        vts_ref[i, v_rows, :] = (vt_own[vrows[hh], :].astype(F32) * w).astype(BF16)
        vts_ref[i, d_rows, :] = jnp.where(row0, w, 0.0).astype(BF16)
        vts_ref[own_slot, v_rows, :] = vt_own[vrows[hh], :]
        vts_ref[own_slot, d_rows, :] = jnp.where(row0, 1.0, 0.0).astype(BF16)

    def softmax_part(src_ref, dst_ref, j, carry):
        out = list(carry)
        for hh in heads:
            m = carry[4 * hh]
            s = src_ref[hh * blk:(hh + 1) * blk, :]
            c_j = slope[hh] * (blk * (j - i) + (blk - 1)).astype(F32)
            chosen = sel_ref[hh, pl.ds(j, 1), :] > 0.5
            m_new = jnp.where(chosen, jnp.maximum(m, jnp.max(s, axis=0, keepdims=True) + c_j), m)
            p = jnp.exp2(s - jnp.where(chosen, m_new - c_j, -NEG_BIG))
            dst_ref[hh * blk:(hh + 1) * blk, :] = p.astype(BF16)
            out[4 * hh] = m_new
            out[4 * hh + 3] = jnp.exp2(m - m_new)
        return tuple(out)

    def value_part(src_ref, j, carry):
        slot = jnp.where(j < 0, own_slot, jnp.minimum(j, i))
        out = list(carry)
        for hh in heads:
            _, l, acc, alpha = carry[4 * hh:4 * hh + 4]
            r = _dot(vts_ref[slot, hh * vt_rows:(hh + 1) * vt_rows, :],
                     src_ref[hh * blk:(hh + 1) * blk, :])
            out[4 * hh + 1] = alpha * l + r[ATTN_HEAD_DIM:ATTN_HEAD_DIM + 1]
            out[4 * hh + 2] = alpha * acc + r[:ATTN_HEAD_DIM]
        return tuple(out)

    sc = (sc0_ref, sc1_ref, sc2_ref)
    pb = (pb0_ref, pb1_ref, pb2_ref)
    store_scores(0, sc[0])
    store_scores(1, sc[1])

    s_own = _dot_nt(kh_own, qs)
    col_bias = key_pos.astype(F32)
    state = []
    for hh in heads:
        s = s_own[hh * blk:(hh + 1) * blk] + slope[hh] * col_bias
        s = jnp.where(key_pos <= qry_pos, s, NEG_BIG)
        m = jnp.max(s, axis=0, keepdims=True)
        pb[2][hh * blk:(hh + 1) * blk, :] = jnp.exp2(s - m).astype(BF16)
        state += [m, jnp.zeros_like(m), jnp.zeros((ATTN_HEAD_DIM, blk), F32), jnp.ones_like(m)]

    def body(t, carry):
        for c in range(3):
            k = 3 * t + c
            store_scores(k + 2, sc[(c + 2) % 3])
            carry = value_part(pb[(c + 2) % 3], k - 1, carry)
            carry = softmax_part(sc[c], pb[c], k, carry)
        return carry

    trips = (i + 2) // 3
    state = lax.fori_loop(0, trips, body, tuple(state))
    state = value_part(pb[2], 3 * trips - 1, state)
    outs = [state[4 * hh + 2] / state[4 * hh + 1] for hh in heads]
    o_ref[...] = jnp.concatenate(outs, axis=0).T.astype(o_ref.dtype)
    km_ref[pl.ds(i, 1), :] = jnp.mean(k_own.astype(F32), axis=0, keepdims=True)


def _moba(slopes, proj, vt3, batch, seq):
    t = proj.shape[0]
    nq = seq // MOBA_BLOCK
    n_pairs = ATTN_N_HEADS // HEADS_PER_LANE_TILE
    q_blk0 = Q_OFF // LANES
    k_blk0 = K_OFF // LANES
    stacked = HEADS_PER_LANE_TILE * MOBA_BLOCK
    return pl.pallas_call(
        _moba_kernel,
        grid=(batch, n_pairs, nq),
        in_specs=[
            pl.BlockSpec(memory_space=pltpu.SMEM),
            pl.BlockSpec((MOBA_BLOCK, LANES), lambda b, hp, i: (b * nq + i, q_blk0 + hp)),
            pl.BlockSpec((MOBA_BLOCK, LANES), lambda b, hp, i: (b * nq + i, k_blk0 + hp)),
            pl.BlockSpec((1, LANES, MOBA_BLOCK), lambda b, hp, i: (b * nq + i, hp, 0)),
        ],
        out_specs=pl.BlockSpec((MOBA_BLOCK, LANES), lambda b, hp, i: (b * nq + i, hp)),
        out_shape=jax.ShapeDtypeStruct((t, ATTN_WIDTH), BF16),
        scratch_shapes=[
            pltpu.VMEM((nq, LANES), F32),
            pltpu.VMEM((nq, stacked, LANES), BF16),
            pltpu.VMEM((nq + 1, HEADS_PER_LANE_TILE * (ATTN_HEAD_DIM + 16), MOBA_BLOCK), BF16),
            pltpu.VMEM((HEADS_PER_LANE_TILE, nq + 8, MOBA_BLOCK), F32),
        ] + [pltpu.VMEM((stacked, MOBA_BLOCK), F32)] * 3 + [pltpu.VMEM((stacked, MOBA_BLOCK), BF16)] * 3,
        compiler_params=pltpu.CompilerParams(
            dimension_semantics=("arbitrary", "arbitrary", "arbitrary"), vmem_limit_bytes=VMEM_LIMIT),
        name="moba",
    )(slopes, proj, proj, vt3)


def _conv_silu(x, carry, w, b):
    y = w[3:4, :] * x + b
    for k in range(1, SSD_CONV):
        y = y + w[SSD_CONV - 1 - k:SSD_CONV - k, :] * _shift_rows(x, k, carry)
    return _silu(y)


def _ssd_kernel(z_ref, xs_ref, b_ref, c_ref, dt_ref, cw_ref, cb_ref, dtb_ref, alog_ref, dsk_ref, nw_ref,
                e_ref, y_ref, st_ref, cx_ref, cbc_ref):
    L = SSD_CHUNK

    @pl.when(pl.program_id(1) == 0)
    def _():
        st_ref[...] = jnp.zeros_like(st_ref)
        cx_ref[...] = jnp.zeros_like(cx_ref)
        cbc_ref[...] = jnp.zeros_like(cbc_ref)

    xs_raw = xs_ref[...].astype(F32)
    bc_raw = jnp.concatenate([b_ref[...], c_ref[...]], axis=1).astype(F32)
    xs = _conv_silu(xs_raw, cx_ref[...], cw_ref[:, :SSD_D_INNER], cb_ref[:, :SSD_D_INNER])
    bc = _conv_silu(bc_raw, cbc_ref[...], cw_ref[:, SSD_D_INNER:], cb_ref[:, SSD_D_INNER:])
    cx_ref[...] = xs_raw[L - 8:, :]
    cbc_ref[...] = bc_raw[L - 8:, :]
    bm = bc[:, :SSD_GN]
    cm = bc[:, SSD_GN:]

    xdt_in = dt_ref[...] + dtb_ref[...]
    dt = jnp.maximum(xdt_in, 0.0) + jnp.log1p(jnp.exp(-jnp.abs(xdt_in)))
    adt = dt * (-jnp.exp(alog_ref[...]))
    row = lax.broadcasted_iota(jnp.int32, (L, L), 0)
    col = lax.broadcasted_iota(jnp.int32, (L, L), 1)
    causal = row >= col
    tri = causal.astype(BF16)
    a_hi, a_mid, a_lo = _split3(adt)
    acs = _dot(tri, a_hi) + _dot(tri, a_mid) + _dot(tri, a_lo)
    acs_t = acs.T
    eacs = jnp.exp(acs)
    wend = jnp.exp(acs[L - 1:L, :] - acs)

    stacked = jnp.concatenate([dt, eacs, wend], axis=0)
    s_hi, s_mid, s_lo = _split3(stacked)
    e = e_ref[...]
    expanded = _dot(s_hi, e) + _dot(s_mid, e) + _dot(s_lo, e)
    dt_x = expanded[:L]
    eacs_x = expanded[L:2 * L]
    wend_x = expanded[2 * L:]

    xdt = xs * dt_x
    xdt_b = xdt.astype(BF16)
    xw_b = (xdt * wend_x).astype(BF16)
    lane = lax.broadcasted_iota(jnp.int32, (L, LANES), 1)
    first_head = lane < SSD_HEAD_DIM
    z = z_ref[...].astype(F32)

    for g in range(SSD_N_GROUPS):
        gl = slice(g * SSD_GROUP_W, (g + 1) * SSD_GROUP_W)
        bg = bm[:, g * SSD_D_STATE:(g + 1) * SSD_D_STATE]
        cg = cm[:, g * SSD_D_STATE:(g + 1) * SSD_D_STATE].astype(BF16)
        cb = _dot_nt(cg, bg.astype(BF16))
        y_inter = _dot(cg, st_ref[:, gl].astype(BF16))
        pairs = []
        for pr in range(SSD_GROUP_W // LANES):
            h0 = (g * SSD_GROUP_W + pr * LANES) // SSD_HEAD_DIM
            xp = xdt_b[:, g * SSD_GROUP_W + pr * LANES:g * SSD_GROUP_W + (pr + 1) * LANES]
            ys = []
            for h in (h0, h0 + 1):
                seg = acs[:, h:h + 1] - acs_t[h:h + 1, :]
                decay = jnp.exp(jnp.where(causal, seg, NEG_BIG))
                ys.append(_dot((cb * decay).astype(BF16), xp))
            pairs.append(jnp.where(first_head, ys[0], ys[1]))
        y = jnp.concatenate(pairs, axis=1) + y_inter * eacs_x[:, gl] + dsk_ref[:, gl] * xs[:, gl]
        st_ref[:, gl] = st_ref[:, gl] * eacs_x[L - 1:L, gl] + _dot(bg.T.astype(BF16), xw_b[:, gl])
        y = y * _silu(z[:, gl])
        y_ref[:, gl] = _rms(y, nw_ref[:, gl]).astype(y_ref.dtype)


def _ssd(proj, dt_raw, cw, cb, dtb, alog, dsk, nw, expand, batch, seq):
    t = proj.shape[0]
    nc = seq // SSD_CHUNK
    L = SSD_CHUNK
    conv_dim = SSD_D_INNER + 2 * SSD_GN

    def rows(b, c):
        return b * nc + c

    full = lambda shape: pl.BlockSpec(shape, lambda b, c: (0, 0))
    return pl.pallas_call(
        _ssd_kernel,
        grid=(batch, nc),
        in_specs=[
            pl.BlockSpec((L, SSD_D_INNER), lambda b, c: (rows(b, c), Z_OFF // SSD_D_INNER)),
            pl.BlockSpec((L, SSD_D_INNER), lambda b, c: (rows(b, c), XS_OFF // SSD_D_INNER)),
            pl.BlockSpec((L, SSD_GN), lambda b, c: (rows(b, c), B_OFF // SSD_GN)),
            pl.BlockSpec((L, SSD_GN), lambda b, c: (rows(b, c), C_OFF // SSD_GN)),
            pl.BlockSpec((L, LANES), lambda b, c: (rows(b, c), 0)),
            full((SSD_CONV, conv_dim)),
            full((1, conv_dim)),
            full((1, LANES)),
            full((1, LANES)),
            full((1, SSD_D_INNER)),
            full((1, SSD_D_INNER)),
            full((LANES, SSD_D_INNER)),
        ],
        out_specs=pl.BlockSpec((L, SSD_D_INNER), lambda b, c: (rows(b, c), 0)),
        out_shape=jax.ShapeDtypeStruct((t, SSD_D_INNER), BF16),
        scratch_shapes=[
            pltpu.VMEM((SSD_D_STATE, SSD_D_INNER), F32),
            pltpu.VMEM((8, SSD_D_INNER), F32),
            pltpu.VMEM((8, 2 * SSD_GN), F32),
        ],
        compiler_params=pltpu.CompilerParams(
            dimension_semantics=("arbitrary", "arbitrary"), vmem_limit_bytes=VMEM_LIMIT),
        name="ssd",
    )(proj, proj, proj, proj, dt_raw, cw, cb, dtb, alog, dsk, nw, expand)


def _mix_kernel(yn_ref, att_ref, gs_ref, ga_ref, x_ref, wssd_ref, wattn_ref, wout_ref, nw_ref, o_ref):
    y_ssd = _dot(yn_ref[...], wssd_ref[...])
    y_attn = _dot(att_ref[...], wattn_ref[...])
    mixed = _sigmoid(gs_ref[...].astype(F32)) * y_ssd + _sigmoid(ga_ref[...].astype(F32)) * y_attn
    o = _dot(mixed.astype(BF16), wout_ref[...])
    o_ref[...] = x_ref[...] + _rms(o, nw_ref[...])


def _mix(ynorm, att, proj, x2, wssd, wattn, wout, nw, tm=512):
    t = x2.shape[0]
    full = lambda shape: pl.BlockSpec(shape, lambda i: (0, 0))
    return pl.pallas_call(
        _mix_kernel,
        grid=(t // tm,),
        in_specs=[
            pl.BlockSpec((tm, SSD_D_INNER), lambda i: (i, 0)),
            pl.BlockSpec((tm, ATTN_WIDTH), lambda i: (i, 0)),
            pl.BlockSpec((tm, D_MODEL), lambda i: (i, GS_OFF // D_MODEL)),
            pl.BlockSpec((tm, D_MODEL), lambda i: (i, GA_OFF // D_MODEL)),
            pl.BlockSpec((tm, D_MODEL), lambda i: (i, 0)),
            full((SSD_D_INNER, D_MODEL)),
            full((ATTN_WIDTH, D_MODEL)),
            full((D_MODEL, D_MODEL)),
            full((1, D_MODEL)),
        ],
        out_specs=pl.BlockSpec((tm, D_MODEL), lambda i: (i, 0)),
        out_shape=jax.ShapeDtypeStruct((t, D_MODEL), F32),
        compiler_params=pltpu.CompilerParams(
            dimension_semantics=("arbitrary",), vmem_limit_bytes=VMEM_LIMIT),
        name="mix",
    )(ynorm, att, proj, proj, x2, wssd, wattn, wout, nw)


def _ffn_kernel(x_ref, nw1_ref, wg_ref, wu_ref, cw_ref, cb_ref, wd_ref, nw2_ref, o_ref,
                h_ref, acc_ref, carry_ref, *, tiles_per_seq):
    i = pl.program_id(0)
    k = pl.program_id(1)
    tm = x_ref.shape[0]

    @pl.when((i == 0) & (k == 0))
    def _():
        carry_ref[...] = jnp.zeros_like(carry_ref)

    @pl.when(k == 0)
    def _():
        h_ref[...] = _rms(x_ref[...], nw1_ref[...]).astype(BF16)
        acc_ref[...] = jnp.zeros_like(acc_ref)

    h = h_ref[...]
    gate = _dot(h, wg_ref[...])
    up = _dot(h, wu_ref[...])
    seq_start = (i % tiles_per_seq) == 0
    carry = jnp.where(seq_start, 0.0, carry_ref[k])
    carry_ref[k] = gate[tm - 8:, :]
    cw = cw_ref[...]
    conv = cw[FFN_CONV - 1:FFN_CONV, :] * gate + cb_ref[...]
    for s in range(1, FFN_CONV):
        conv = conv + cw[FFN_CONV - 1 - s:FFN_CONV - s, :] * _shift_rows(gate, s, carry)
    c0 = 0.7978845608028654
    gelu = 0.5 * conv * (1.0 + jnp.tanh(c0 * (conv + 0.044715 * (conv * conv * conv))))
    acc_ref[...] += _dot((gelu * up).astype(BF16), wd_ref[...])

    @pl.when(k == pl.num_programs(1) - 1)
    def _():
        o_ref[...] = x_ref[...] + _rms(acc_ref[...], nw2_ref[...])


def _ffn(x1, nw1, wup, cw, cb, wd, nw2, seq, tm=1024, tk=512):
    t = x1.shape[0]
    nk = FFN_HIDDEN // tk
    kern = functools.partial(_ffn_kernel, tiles_per_seq=seq // tm)
    return pl.pallas_call(
        kern,
        grid=(t // tm, nk),
        in_specs=[
            pl.BlockSpec((tm, D_MODEL), lambda i, k: (i, 0)),
            pl.BlockSpec((1, D_MODEL), lambda i, k: (0, 0)),
            pl.BlockSpec((D_MODEL, tk), lambda i, k: (0, k)),
            pl.BlockSpec((D_MODEL, tk), lambda i, k: (0, nk + k)),
            pl.BlockSpec((FFN_CONV, tk), lambda i, k: (0, k)),
            pl.BlockSpec((1, tk), lambda i, k: (0, k)),
            pl.BlockSpec((tk, D_MODEL), lambda i, k: (k, 0)),
            pl.BlockSpec((1, D_MODEL), lambda i, k: (0, 0)),
        ],
        out_specs=pl.BlockSpec((tm, D_MODEL), lambda i, k: (i, 0)),
        out_shape=jax.ShapeDtypeStruct((t, D_MODEL), F32),
        scratch_shapes=[
            pltpu.VMEM((tm, D_MODEL), BF16),
            pltpu.VMEM((tm, D_MODEL), F32),
            pltpu.VMEM((nk, 8, tk), F32),
        ],
        compiler_params=pltpu.CompilerParams(
            dimension_semantics=("arbitrary", "arbitrary"), vmem_limit_bytes=VMEM_LIMIT),
        name="ffn",
    )(x1, nw1, wup, wup, cw, cb, wd, nw2)


def _pad_lanes(v):
    return jnp.pad(v.astype(F32), (0, LANES - v.shape[0]))[None, :]


def kernel(x, pre_mix_norm, w_in, ssd_conv_w, ssd_conv_b, ssd_dt_bias, ssd_a_log, ssd_d_skip, ssd_out_norm,
           w_ssd_branch, w_attn_branch, w_out, post_mix_norm, pre_ffn_norm, w_ffn_up, ffn_conv_w, ffn_conv_b,
           w_ffn_down, post_ffn_norm):
    batch, seq, d = x.shape
    assert d == D_MODEL and seq % 1024 == 0
    depth = w_in.shape[0]
    x2 = x.reshape(batch * seq, d)

    slopes = 2.0 ** (-8.0 * jnp.arange(1, ATTN_N_HEADS + 1, dtype=F32) / ATTN_N_HEADS)
    head_of_channel = jnp.arange(SSD_D_INNER) // SSD_HEAD_DIM
    expand = (jnp.arange(LANES)[:, None] == head_of_channel[None, :]).astype(BF16)
    st = _IN_STARTS

    for li in range(depth):
        w = w_in[li]
        wp = jnp.concatenate([w[:, st[0]:st[2]], w[:, st[3]:st[5]], w[:, st[6]:]], axis=1).astype(BF16)
        wdt = jnp.pad(w[:, st[2]:st[3]], ((0, 0), (0, LANES - SSD_N_HEADS))).astype(BF16)
        wvt = w[:, st[5]:st[6]].T.astype(BF16)
        nw_mix = pre_mix_norm[li][None, :]

        proj, dt_raw = _inproj(x2, nw_mix, wp, wdt)
        vt3 = _vproj_t(x2, nw_mix, wvt)
        att = _moba(slopes, proj, vt3, batch, seq)
        ynorm = _ssd(proj, dt_raw, ssd_conv_w[li], ssd_conv_b[li][None, :], _pad_lanes(ssd_dt_bias[li]),
                     _pad_lanes(ssd_a_log[li]), jnp.repeat(ssd_d_skip[li].astype(F32), SSD_HEAD_DIM)[None, :],
                     ssd_out_norm[li][None, :], expand, batch, seq)
        x1 = _mix(ynorm, att, proj, x2, w_ssd_branch[li].astype(BF16), w_attn_branch[li].astype(BF16),
                  w_out[li].astype(BF16), post_mix_norm[li][None, :])
        x2 = _ffn(x1, pre_ffn_norm[li][None, :], w_ffn_up[li].astype(BF16), ffn_conv_w[li],
                  ffn_conv_b[li][None, :], w_ffn_down[li].astype(BF16), post_ffn_norm[li][None, :], seq)
    return x2.reshape(batch, seq, d)
```

```python
import functools

import jax
import jax.numpy as jnp
from jax import lax
from jax.experimental import pallas as pl
from jax.experimental.pallas import tpu as pltpu

F32 = jnp.float32
BF16 = jnp.bfloat16

D_MODEL = 1024
SSD_D_INNER = 2048
SSD_HEAD_DIM = 64
SSD_N_HEADS = 32
SSD_N_GROUPS = 4
SSD_D_STATE = 128
SSD_CONV = 4
SSD_CHUNK = 128
SSD_GN = SSD_N_GROUPS * SSD_D_STATE
SSD_GROUP_W = SSD_D_INNER // SSD_N_GROUPS
ATTN_HEAD_DIM = 64
ATTN_N_HEADS = 16
ATTN_WIDTH = 1024
MOBA_BLOCK = 256
MOBA_TOPK = 3
MOBA_RING = 4
FFN_HIDDEN = 4096
FFN_CONV = 3
NORM_EPS = 1e-6

LANES = 128
HEADS_PER_LANE_TILE = LANES // ATTN_HEAD_DIM
NEG_BIG = -1e30
LOG2E = 1.4426950408889634

_IN_SIZES = (SSD_D_INNER, SSD_D_INNER + 2 * SSD_GN, SSD_N_HEADS, ATTN_WIDTH, ATTN_WIDTH, ATTN_WIDTH,
             D_MODEL, D_MODEL)
_IN_STARTS = tuple(sum(_IN_SIZES[:j]) for j in range(len(_IN_SIZES)))
Z_OFF = 0
XS_OFF = 2048
B_OFF = 4096
C_OFF = 4608
Q_OFF = 5120
K_OFF = 6144
GS_OFF = 7168
GA_OFF = 8192
PROJ_COLS = 9216

VMEM_LIMIT = 48 * 1024 * 1024


def _dot(a, b):
    return jnp.dot(a, b, preferred_element_type=F32)


def _dot_nt(a, b):
    return lax.dot_general(a, b, (((1,), (1,)), ((), ())), preferred_element_type=F32)


def _rms(x, w):
    return x * lax.rsqrt(jnp.mean(x * x, axis=-1, keepdims=True) + NORM_EPS) * w


def _sigmoid(x):
    return 1.0 / (1.0 + jnp.exp(-x))


def _silu(x):
    return x * _sigmoid(x)


def _split3(v):
    hi = v.astype(BF16)
    r1 = v - hi.astype(F32)
    mid = r1.astype(BF16)
    lo = (r1 - mid.astype(F32)).astype(BF16)
    return hi, mid, lo


def _shift_rows(x, k, carry):
    r = pltpu.roll(x, k, axis=0)
    hist = carry.shape[0]
    row = lax.broadcasted_iota(jnp.int32, carry.shape, 0)
    top = jnp.where(row < k, pltpu.roll(carry, k, axis=0), r[:hist])
    return jnp.concatenate([top, r[hist:]], axis=0)


def _inproj_kernel(x_ref, nw_ref, w_ref, wdt_ref, proj_ref, dt_ref, h_ref):
    @pl.when(pl.program_id(1) == 0)
    def _():
        h = _rms(x_ref[...], nw_ref[...]).astype(BF16)
        h_ref[...] = h
        dt_ref[...] = _dot(h, wdt_ref[...])

    proj_ref[...] = _dot(h_ref[...], w_ref[...]).astype(BF16)


def _inproj(x2, nw, wp, wdt, tm=1024, tn=1024):
    t = x2.shape[0]
    return pl.pallas_call(
        _inproj_kernel,
        grid=(t // tm, PROJ_COLS // tn),
        in_specs=[
            pl.BlockSpec((tm, D_MODEL), lambda i, j: (i, 0)),
            pl.BlockSpec((1, D_MODEL), lambda i, j: (0, 0)),
            pl.BlockSpec((D_MODEL, tn), lambda i, j: (0, j)),
            pl.BlockSpec((D_MODEL, LANES), lambda i, j: (0, 0)),
        ],
        out_specs=[
            pl.BlockSpec((tm, tn), lambda i, j: (i, j)),
            pl.BlockSpec((tm, LANES), lambda i, j: (i, 0)),
        ],
        out_shape=[
            jax.ShapeDtypeStruct((t, PROJ_COLS), BF16),
            jax.ShapeDtypeStruct((t, LANES), F32),
        ],
        scratch_shapes=[pltpu.VMEM((tm, D_MODEL), BF16)],
        compiler_params=pltpu.CompilerParams(
            dimension_semantics=("arbitrary", "arbitrary"), vmem_limit_bytes=VMEM_LIMIT),
        name="inproj",
    )(x2, nw, wp, wdt)


def _vt_kernel(x_ref, nw_ref, wvt_ref, vt_ref):
    h = _rms(x_ref[...], nw_ref[...]).astype(BF16)
    vt = _dot_nt(wvt_ref[...], h)
    for c in range(vt_ref.shape[0]):
        vt_ref[c] = vt[:, c * MOBA_BLOCK:(c + 1) * MOBA_BLOCK].astype(BF16)


def _vproj_t(x2, nw, wvt, tm=512):
    t = x2.shape[0]
    return pl.pallas_call(
        _vt_kernel,
        grid=(t // tm,),
        in_specs=[
            pl.BlockSpec((tm, D_MODEL), lambda i: (i, 0)),
            pl.BlockSpec((1, D_MODEL), lambda i: (0, 0)),
            pl.BlockSpec((ATTN_WIDTH, D_MODEL), lambda i: (0, 0)),
        ],
        out_specs=pl.BlockSpec((tm // MOBA_BLOCK, ATTN_WIDTH, MOBA_BLOCK), lambda i: (i, 0, 0)),
        out_shape=jax.ShapeDtypeStruct((t // MOBA_BLOCK, ATTN_WIDTH, MOBA_BLOCK), BF16),
        compiler_params=pltpu.CompilerParams(
            dimension_semantics=("arbitrary",), vmem_limit_bytes=VMEM_LIMIT),
        name="vproj_t",
    )(x2, nw, wvt)


def _moba_kernel(slopes_ref, q_ref, k_ref, vt_ref, o_ref, km_ref, kh_ref, vts_ref, sel_ref, *ring_refs):
    hp = pl.program_id(1)
    i = pl.program_id(2)
    nb = km_ref.shape[0]
    blk = MOBA_BLOCK
    heads = range(HEADS_PER_LANE_TILE)
    den_rows = 16
    vt_rows = ATTN_HEAD_DIM + den_rows

    @pl.when(i == 0)
    def _():
        km_ref[...] = jnp.zeros_like(km_ref)

    q_raw = q_ref[...]
    qs = (q_raw.astype(F32) * (ATTN_HEAD_DIM ** -0.5 * LOG2E)).astype(BF16)
    lane = lax.broadcasted_iota(jnp.int32, (1, LANES), 1)
    key_pos = lax.broadcasted_iota(jnp.int32, (blk, blk), 0)
    qry_pos = lax.broadcasted_iota(jnp.int32, (blk, blk), 1)
    key_lane = lax.broadcasted_iota(jnp.int32, (1, blk), 1)
    blk_id = lax.broadcasted_iota(jnp.int32, (nb, blk), 0)
    k_own = k_ref[...]
    vt_own = vt_ref[0]
    valid = blk_id < i

    head_lanes, vrows, slope = [], [], []
    for hh in heads:
        lo_lane = hh * ATTN_HEAD_DIM
        head_lanes.append((lane >= lo_lane) & (lane < lo_lane + ATTN_HEAD_DIM))
        vrows.append(slice(lo_lane, lo_lane + ATTN_HEAD_DIM))
        slope.append(slopes_ref[hp * HEADS_PER_LANE_TILE + hh] * LOG2E)

    kh_own = jnp.concatenate([jnp.where(head_lanes[hh], k_own, jnp.zeros_like(k_own)) for hh in heads], axis=0)
    kh_ref[i] = kh_own

    km = km_ref[...]
    gate_lhs = jnp.concatenate(
        [part for hh in heads for part in _split3(jnp.where(head_lanes[hh], km, 0.0))], axis=0)
    gate_all = _dot_nt(gate_lhs, q_raw)
    for hh in heads:
        gate = sum(gate_all[(3 * hh + c) * nb:(3 * hh + c + 1) * nb] for c in range(3))
        g = jnp.where(valid, gate, -jnp.inf)
        sel = jnp.zeros((nb, blk), jnp.bool_)
        for _ in range(MOBA_TOPK):
            top = jnp.max(g, axis=0, keepdims=True)
            first = jnp.min(jnp.where(g == top, blk_id, nb), axis=0, keepdims=True)
            pick = blk_id == first
            sel = sel | (pick & (top > -jnp.inf))
            g = jnp.where(pick, -jnp.inf, g)
        sel_ref[hh, :nb, :] = sel.astype(F32)
        sel_ref[hh, nb:, :] = jnp.zeros((sel_ref.shape[1] - nb, blk), F32)

    def store_scores(j, dst_ref):
        dst_ref[...] = _dot_nt(kh_ref[jnp.minimum(j, i)], qs)

    row0 = lax.broadcasted_iota(jnp.int32, (den_rows, blk), 0) == 0
    own_slot = nb
    for hh in heads:
        w = jnp.exp2(slope[hh] * (key_lane - (blk - 1)).astype(F32))
        v_rows = slice(hh * vt_rows, hh * vt_rows + ATTN_HEAD_DIM)
        d_rows = slice(hh * vt_rows + ATTN_HEAD_DIM, (hh + 1) * vt_rows)
        vts_ref[i, v_rows, :] = (vt_own[vrows[hh], :].astype(F32) * w).astype(BF16)
        vts_ref[i, d_rows, :] = jnp.where(row0, w, 0.0).astype(BF16)
        vts_ref[own_slot, v_rows, :] = vt_own[vrows[hh], :]
        vts_ref[own_slot, d_rows, :] = jnp.where(row0, 1.0, 0.0).astype(BF16)

    def softmax_part(src_ref, dst_ref, j, carry):
        out = list(carry)
        for hh in heads:
            m = carry[4 * hh]
            s = src_ref[hh * blk:(hh + 1) * blk, :]
            c_j = slope[hh] * (blk * (j - i) + (blk - 1)).astype(F32)
            chosen = sel_ref[hh, pl.ds(j, 1), :] > 0.5
            m_new = jnp.where(chosen, jnp.maximum(m, jnp.max(s, axis=0, keepdims=True) + c_j), m)
            p = jnp.exp2(s - jnp.where(chosen, m_new - c_j, -NEG_BIG))
            dst_ref[hh * blk:(hh + 1) * blk, :] = p.astype(BF16)
            out[4 * hh] = m_new
            out[4 * hh + 3] = jnp.exp2(m - m_new)
        return tuple(out)

    def value_part(src_ref, j, carry):
        slot = jnp.where(j < 0, own_slot, jnp.minimum(j, i))
        out = list(carry)
        for hh in heads:
            _, l, acc, alpha = carry[4 * hh:4 * hh + 4]
            r = _dot(vts_ref[slot, hh * vt_rows:(hh + 1) * vt_rows, :],
                     src_ref[hh * blk:(hh + 1) * blk, :])
            out[4 * hh + 1] = alpha * l + r[ATTN_HEAD_DIM:ATTN_HEAD_DIM + 1]
            out[4 * hh + 2] = alpha * acc + r[:ATTN_HEAD_DIM]
        return tuple(out)

    ring = MOBA_RING
    sc = ring_refs[:ring]
    pb = ring_refs[ring:]
    store_scores(0, sc[0])
    store_scores(1, sc[1])

    s_own = _dot_nt(kh_own, qs)
    col_bias = key_pos.astype(F32)
    state = []
    for hh in heads:
        s = s_own[hh * blk:(hh + 1) * blk] + slope[hh] * col_bias
        s = jnp.where(key_pos <= qry_pos, s, NEG_BIG)
        m = jnp.max(s, axis=0, keepdims=True)
        pb[ring - 1][hh * blk:(hh + 1) * blk, :] = jnp.exp2(s - m).astype(BF16)
        state += [m, jnp.zeros_like(m), jnp.zeros((ATTN_HEAD_DIM, blk), F32), jnp.ones_like(m)]

    def body(t, carry):
        for c in range(ring):
            k = ring * t + c
            store_scores(k + 2, sc[(c + 2) % ring])
            carry = value_part(pb[(c - 1) % ring], k - 1, carry)
            carry = softmax_part(sc[c], pb[c], k, carry)
        return carry

    trips = (i + ring - 1) // ring
    state = lax.fori_loop(0, trips, body, tuple(state))
    state = value_part(pb[ring - 1], ring * trips - 1, state)
    outs = [state[4 * hh + 2] / state[4 * hh + 1] for hh in heads]
    o_ref[...] = jnp.concatenate(outs, axis=0).T.astype(o_ref.dtype)
    km_ref[pl.ds(i, 1), :] = jnp.mean(k_own.astype(F32), axis=0, keepdims=True)


def _moba(slopes, proj, vt3, batch, seq):
    t = proj.shape[0]
    nq = seq // MOBA_BLOCK
    n_pairs = ATTN_N_HEADS // HEADS_PER_LANE_TILE
    q_blk0 = Q_OFF // LANES
    k_blk0 = K_OFF // LANES
    stacked = HEADS_PER_LANE_TILE * MOBA_BLOCK
    return pl.pallas_call(
        _moba_kernel,
        grid=(batch, n_pairs, nq),
        in_specs=[
            pl.BlockSpec(memory_space=pltpu.SMEM),
            pl.BlockSpec((MOBA_BLOCK, LANES), lambda b, hp, i: (b * nq + i, q_blk0 + hp)),
            pl.BlockSpec((MOBA_BLOCK, LANES), lambda b, hp, i: (b * nq + i, k_blk0 + hp)),
            pl.BlockSpec((1, LANES, MOBA_BLOCK), lambda b, hp, i: (b * nq + i, hp, 0)),
        ],
        out_specs=pl.BlockSpec((MOBA_BLOCK, LANES), lambda b, hp, i: (b * nq + i, hp)),
        out_shape=jax.ShapeDtypeStruct((t, ATTN_WIDTH), BF16),
        scratch_shapes=[
            pltpu.VMEM((nq, LANES), F32),
            pltpu.VMEM((nq, stacked, LANES), BF16),
            pltpu.VMEM((nq + 1, HEADS_PER_LANE_TILE * (ATTN_HEAD_DIM + 16), MOBA_BLOCK), BF16),
            pltpu.VMEM((HEADS_PER_LANE_TILE, nq + 8, MOBA_BLOCK), F32),
        ] + [pltpu.VMEM((stacked, MOBA_BLOCK), F32)] * MOBA_RING + [pltpu.VMEM((stacked, MOBA_BLOCK), BF16)] * MOBA_RING,
        compiler_params=pltpu.CompilerParams(
            dimension_semantics=("arbitrary", "arbitrary", "arbitrary"), vmem_limit_bytes=VMEM_LIMIT),
        name="moba",
    )(slopes, proj, proj, vt3)


def _conv_silu(x, carry, w, b):
    y = w[3:4, :] * x + b
    for k in range(1, SSD_CONV):
        y = y + w[SSD_CONV - 1 - k:SSD_CONV - k, :] * _shift_rows(x, k, carry)
    return _silu(y)


def _ssd_kernel(z_ref, xs_ref, b_ref, c_ref, dt_ref, cw_ref, cb_ref, dtb_ref, alog_ref, dsk_ref, nw_ref,
                e_ref, y_ref, st_ref, cx_ref, cbc_ref):
    L = SSD_CHUNK

    @pl.when(pl.program_id(1) == 0)
    def _():
        st_ref[...] = jnp.zeros_like(st_ref)
        cx_ref[...] = jnp.zeros_like(cx_ref)
        cbc_ref[...] = jnp.zeros_like(cbc_ref)

    def conv_silu(src_ref, hist_ref, cols, hist_cols, w_cols):
        raw = src_ref[:, cols].astype(F32)
        out = _conv_silu(raw, hist_ref[:, hist_cols], cw_ref[:, w_cols], cb_ref[:, w_cols])
        hist_ref[:, hist_cols] = raw[L - 8:, :]
        return out

    xdt_in = dt_ref[...] + dtb_ref[...]
    dt = jnp.maximum(xdt_in, 0.0) + jnp.log1p(jnp.exp(-jnp.abs(xdt_in)))
    adt = dt * (-jnp.exp(alog_ref[...]))
    row = lax.broadcasted_iota(jnp.int32, (L, L), 0)
    col = lax.broadcasted_iota(jnp.int32, (L, L), 1)
    causal = row >= col
    tri = causal.astype(BF16)
    a_hi, a_mid, a_lo = _split3(adt)
    acs = _dot(tri, a_hi) + _dot(tri, a_mid) + _dot(tri, a_lo)
    acs_t = acs.T
    eacs = jnp.exp(acs)
    wend = jnp.exp(acs[L - 1:L, :] - acs)

    stacked = jnp.concatenate([dt, eacs, wend], axis=0)
    s_parts = _split3(stacked)
    lane = lax.broadcasted_iota(jnp.int32, (L, LANES), 1)
    first_head = lane < SSD_HEAD_DIM

    for g in range(SSD_N_GROUPS):
        gl = slice(g * SSD_GROUP_W, (g + 1) * SSD_GROUP_W)
        nl = slice(g * SSD_D_STATE, (g + 1) * SSD_D_STATE)
        cl = slice(SSD_GN + g * SSD_D_STATE, SSD_GN + (g + 1) * SSD_D_STATE)
        e = e_ref[:, gl]
        expanded = sum(_dot(part, e) for part in s_parts)
        dt_x = expanded[:L]
        eacs_x = expanded[L:2 * L]
        wend_x = expanded[2 * L:]

        xs = conv_silu(xs_ref, cx_ref, gl, gl, gl)
        bg = conv_silu(b_ref, cbc_ref, nl, nl, slice(SSD_D_INNER + nl.start, SSD_D_INNER + nl.stop))
        cg = conv_silu(c_ref, cbc_ref, nl, cl, slice(SSD_D_INNER + cl.start, SSD_D_INNER + cl.stop)).astype(BF16)
        xdt = xs * dt_x
        xdt_b = xdt.astype(BF16)
        xw_b = (xdt * wend_x).astype(BF16)

        cb = _dot_nt(cg, bg.astype(BF16))
        y_inter = _dot(cg, st_ref[:, gl].astype(BF16))
        pairs = []
        for pr in range(SSD_GROUP_W // LANES):
            h0 = (g * SSD_GROUP_W + pr * LANES) // SSD_HEAD_DIM
            xp = xdt_b[:, pr * LANES:(pr + 1) * LANES]
            ys = []
            for h in (h0, h0 + 1):
                seg = acs[:, h:h + 1] - acs_t[h:h + 1, :]
                decay = jnp.exp(jnp.where(causal, seg, NEG_BIG))
                ys.append(_dot((cb * decay).astype(BF16), xp))
            pairs.append(jnp.where(first_head, ys[0], ys[1]))
        y = jnp.concatenate(pairs, axis=1) + y_inter * eacs_x + dsk_ref[:, gl] * xs
        st_ref[:, gl] = st_ref[:, gl] * eacs_x[L - 1:L, :] + _dot(bg.T.astype(BF16), xw_b)
        y = y * _silu(z_ref[:, gl].astype(F32))
        y_ref[:, gl] = _rms(y, nw_ref[:, gl]).astype(y_ref.dtype)


def _ssd(proj, dt_raw, cw, cb, dtb, alog, dsk, nw, expand, batch, seq):
    t = proj.shape[0]
    nc = seq // SSD_CHUNK
    L = SSD_CHUNK
    conv_dim = SSD_D_INNER + 2 * SSD_GN

    def rows(b, c):
        return b * nc + c

    full = lambda shape: pl.BlockSpec(shape, lambda b, c: (0, 0))
    return pl.pallas_call(
        _ssd_kernel,
        grid=(batch, nc),
        in_specs=[
            pl.BlockSpec((L, SSD_D_INNER), lambda b, c: (rows(b, c), Z_OFF // SSD_D_INNER)),
            pl.BlockSpec((L, SSD_D_INNER), lambda b, c: (rows(b, c), XS_OFF // SSD_D_INNER)),
            pl.BlockSpec((L, SSD_GN), lambda b, c: (rows(b, c), B_OFF // SSD_GN)),
            pl.BlockSpec((L, SSD_GN), lambda b, c: (rows(b, c), C_OFF // SSD_GN)),
            pl.BlockSpec((L, LANES), lambda b, c: (rows(b, c), 0)),
            full((SSD_CONV, conv_dim)),
            full((1, conv_dim)),
            full((1, LANES)),
            full((1, LANES)),
            full((1, SSD_D_INNER)),
            full((1, SSD_D_INNER)),
            full((LANES, SSD_D_INNER)),
        ],
        out_specs=pl.BlockSpec((L, SSD_D_INNER), lambda b, c: (rows(b, c), 0)),
        out_shape=jax.ShapeDtypeStruct((t, SSD_D_INNER), BF16),
        scratch_shapes=[
            pltpu.VMEM((SSD_D_STATE, SSD_D_INNER), F32),
            pltpu.VMEM((8, SSD_D_INNER), F32),
            pltpu.VMEM((8, 2 * SSD_GN), F32),
        ],
        compiler_params=pltpu.CompilerParams(
            dimension_semantics=("arbitrary", "arbitrary"), vmem_limit_bytes=VMEM_LIMIT),
        name="ssd",
    )(proj, proj, proj, proj, dt_raw, cw, cb, dtb, alog, dsk, nw, expand)


def _mix_kernel(yn_ref, att_ref, gs_ref, ga_ref, x_ref, wssd_ref, wattn_ref, wout_ref, nw_ref, o_ref):
    y_ssd = _dot(yn_ref[...], wssd_ref[...])
    y_attn = _dot(att_ref[...], wattn_ref[...])
    mixed = _sigmoid(gs_ref[...].astype(F32)) * y_ssd + _sigmoid(ga_ref[...].astype(F32)) * y_attn
    o = _dot(mixed.astype(BF16), wout_ref[...])
    o_ref[...] = x_ref[...] + _rms(o, nw_ref[...])


def _mix(ynorm, att, proj, x2, wssd, wattn, wout, nw, tm=512):
    t = x2.shape[0]
    full = lambda shape: pl.BlockSpec(shape, lambda i: (0, 0))
    return pl.pallas_call(
        _mix_kernel,
        grid=(t // tm,),
        in_specs=[
            pl.BlockSpec((tm, SSD_D_INNER), lambda i: (i, 0)),
            pl.BlockSpec((tm, ATTN_WIDTH), lambda i: (i, 0)),
            pl.BlockSpec((tm, D_MODEL), lambda i: (i, GS_OFF // D_MODEL)),
            pl.BlockSpec((tm, D_MODEL), lambda i: (i, GA_OFF // D_MODEL)),
            pl.BlockSpec((tm, D_MODEL), lambda i: (i, 0)),
            full((SSD_D_INNER, D_MODEL)),
            full((ATTN_WIDTH, D_MODEL)),
            full((D_MODEL, D_MODEL)),
            full((1, D_MODEL)),
        ],
        out_specs=pl.BlockSpec((tm, D_MODEL), lambda i: (i, 0)),
        out_shape=jax.ShapeDtypeStruct((t, D_MODEL), F32),
        compiler_params=pltpu.CompilerParams(
            dimension_semantics=("arbitrary",), vmem_limit_bytes=VMEM_LIMIT),
        name="mix",
    )(ynorm, att, proj, proj, x2, wssd, wattn, wout, nw)


def _ffn_kernel(x_ref, nw1_ref, wg_ref, wu_ref, cw_ref, cb_ref, wd_ref, nw2_ref, o_ref,
                h_ref, acc_ref, carry_ref, *, tiles_per_seq):
    i = pl.program_id(0)
    k = pl.program_id(1)
    tm = x_ref.shape[0]

    @pl.when((i == 0) & (k == 0))
    def _():
        carry_ref[...] = jnp.zeros_like(carry_ref)

    @pl.when(k == 0)
    def _():
        h_ref[...] = _rms(x_ref[...], nw1_ref[...]).astype(BF16)
        acc_ref[...] = jnp.zeros_like(acc_ref)

    h = h_ref[...]
    gate = _dot(h, wg_ref[...])
    up = _dot(h, wu_ref[...])
    seq_start = (i % tiles_per_seq) == 0
    carry = jnp.where(seq_start, 0.0, carry_ref[k])
    carry_ref[k] = gate[tm - 8:, :]
    cw = cw_ref[...]
    conv = cw[FFN_CONV - 1:FFN_CONV, :] * gate + cb_ref[...]
    for s in range(1, FFN_CONV):
        conv = conv + cw[FFN_CONV - 1 - s:FFN_CONV - s, :] * _shift_rows(gate, s, carry)
    c0 = 0.7978845608028654
    gelu = 0.5 * conv * (1.0 + jnp.tanh(c0 * (conv + 0.044715 * (conv * conv * conv))))
    acc_ref[...] += _dot((gelu * up).astype(BF16), wd_ref[...])

    @pl.when(k == pl.num_programs(1) - 1)
    def _():
        o_ref[...] = x_ref[...] + _rms(acc_ref[...], nw2_ref[...])


def _ffn(x1, nw1, wup, cw, cb, wd, nw2, seq, tm=1024, tk=512):
    t = x1.shape[0]
    nk = FFN_HIDDEN // tk
    kern = functools.partial(_ffn_kernel, tiles_per_seq=seq // tm)
    return pl.pallas_call(
        kern,
        grid=(t // tm, nk),
        in_specs=[
            pl.BlockSpec((tm, D_MODEL), lambda i, k: (i, 0)),
            pl.BlockSpec((1, D_MODEL), lambda i, k: (0, 0)),
            pl.BlockSpec((D_MODEL, tk), lambda i, k: (0, k)),
            pl.BlockSpec((D_MODEL, tk), lambda i, k: (0, nk + k)),
            pl.BlockSpec((FFN_CONV, tk), lambda i, k: (0, k)),
            pl.BlockSpec((1, tk), lambda i, k: (0, k)),
            pl.BlockSpec((tk, D_MODEL), lambda i, k: (k, 0)),
            pl.BlockSpec((1, D_MODEL), lambda i, k: (0, 0)),
        ],
        out_specs=pl.BlockSpec((tm, D_MODEL), lambda i, k: (i, 0)),
        out_shape=jax.ShapeDtypeStruct((t, D_MODEL), F32),
        scratch_shapes=[
            pltpu.VMEM((tm, D_MODEL), BF16),
            pltpu.VMEM((tm, D_MODEL), F32),
            pltpu.VMEM((nk, 8, tk), F32),
        ],
        compiler_params=pltpu.CompilerParams(
            dimension_semantics=("arbitrary", "arbitrary"), vmem_limit_bytes=VMEM_LIMIT),
        name="ffn",
    )(x1, nw1, wup, wup, cw, cb, wd, nw2)


def _pad_lanes(v):
    return jnp.pad(v.astype(F32), (0, LANES - v.shape[0]))[None, :]


def kernel(x, pre_mix_norm, w_in, ssd_conv_w, ssd_conv_b, ssd_dt_bias, ssd_a_log, ssd_d_skip, ssd_out_norm,
           w_ssd_branch, w_attn_branch, w_out, post_mix_norm, pre_ffn_norm, w_ffn_up, ffn_conv_w, ffn_conv_b,
           w_ffn_down, post_ffn_norm):
    batch, seq, d = x.shape
    assert d == D_MODEL and seq % 1024 == 0
    depth = w_in.shape[0]
    x2 = x.reshape(batch * seq, d)

    slopes = 2.0 ** (-8.0 * jnp.arange(1, ATTN_N_HEADS + 1, dtype=F32) / ATTN_N_HEADS)
    head_of_channel = jnp.arange(SSD_D_INNER) // SSD_HEAD_DIM
    expand = (jnp.arange(LANES)[:, None] == head_of_channel[None, :]).astype(BF16)
    st = _IN_STARTS

    for li in range(depth):
        w = w_in[li]
        wp = jnp.concatenate([w[:, st[0]:st[2]], w[:, st[3]:st[5]], w[:, st[6]:]], axis=1).astype(BF16)
        wdt = jnp.pad(w[:, st[2]:st[3]], ((0, 0), (0, LANES - SSD_N_HEADS))).astype(BF16)
        wvt = w[:, st[5]:st[6]].T.astype(BF16)
        nw_mix = pre_mix_norm[li][None, :]

        proj, dt_raw = _inproj(x2, nw_mix, wp, wdt)
        vt3 = _vproj_t(x2, nw_mix, wvt)
        att = _moba(slopes, proj, vt3, batch, seq)
        ynorm = _ssd(proj, dt_raw, ssd_conv_w[li], ssd_conv_b[li][None, :], _pad_lanes(ssd_dt_bias[li]),
                     _pad_lanes(ssd_a_log[li]), jnp.repeat(ssd_d_skip[li].astype(F32), SSD_HEAD_DIM)[None, :],
                     ssd_out_norm[li][None, :], expand, batch, seq)
        x1 = _mix(ynorm, att, proj, x2, w_ssd_branch[li].astype(BF16), w_attn_branch[li].astype(BF16),
                  w_out[li].astype(BF16), post_mix_norm[li][None, :])
        x2 = _ffn(x1, pre_ffn_norm[li][None, :], w_ffn_up[li].astype(BF16), ffn_conv_w[li],
                  ffn_conv_b[li][None, :], w_ffn_down[li].astype(BF16), post_ffn_norm[li][None, :], seq)
    return x2.reshape(batch, seq, d)
```

```python
import functools

import jax
import jax.numpy as jnp
from jax import lax
from jax.experimental import pallas as pl
from jax.experimental.pallas import tpu as pltpu

F32 = jnp.float32
BF16 = jnp.bfloat16

D_MODEL = 1024
SSD_D_INNER = 2048
SSD_HEAD_DIM = 64
SSD_N_HEADS = 32
SSD_N_GROUPS = 4
SSD_D_STATE = 128
SSD_CONV = 4
SSD_CHUNK = 128
SSD_GN = SSD_N_GROUPS * SSD_D_STATE
SSD_GROUP_W = SSD_D_INNER // SSD_N_GROUPS
ATTN_HEAD_DIM = 64
ATTN_N_HEADS = 16
ATTN_WIDTH = 1024
MOBA_BLOCK = 256
MOBA_TOPK = 3
MOBA_RING = 4
MOBA_TILES_PER_STEP = 4
FFN_HIDDEN = 4096
FFN_CONV = 3
NORM_EPS = 1e-6

LANES = 128
HEADS_PER_LANE_TILE = LANES // ATTN_HEAD_DIM
NEG_BIG = -1e30
LOG2E = 1.4426950408889634

_IN_SIZES = (SSD_D_INNER, SSD_D_INNER + 2 * SSD_GN, SSD_N_HEADS, ATTN_WIDTH, ATTN_WIDTH, ATTN_WIDTH,
             D_MODEL, D_MODEL)
_IN_STARTS = tuple(sum(_IN_SIZES[:j]) for j in range(len(_IN_SIZES)))
Z_OFF = 0
XS_OFF = 2048
B_OFF = 4096
C_OFF = 4608
Q_OFF = 5120
K_OFF = 6144
GS_OFF = 7168
GA_OFF = 8192
PROJ_COLS = 9216

VMEM_LIMIT = 48 * 1024 * 1024


def _dot(a, b):
    return jnp.dot(a, b, preferred_element_type=F32)


def _dot_nt(a, b):
    return lax.dot_general(a, b, (((1,), (1,)), ((), ())), preferred_element_type=F32)


def _rms(x, w):
    return x * lax.rsqrt(jnp.mean(x * x, axis=-1, keepdims=True) + NORM_EPS) * w


def _sigmoid(x):
    return 1.0 / (1.0 + jnp.exp(-x))


def _silu(x):
    return x * _sigmoid(x)


def _split3(v):
    hi = v.astype(BF16)
    r1 = v - hi.astype(F32)
    mid = r1.astype(BF16)
    lo = (r1 - mid.astype(F32)).astype(BF16)
    return hi, mid, lo


def _shift_rows(x, k, carry):
    r = pltpu.roll(x, k, axis=0)
    hist = carry.shape[0]
    row = lax.broadcasted_iota(jnp.int32, carry.shape, 0)
    top = jnp.where(row < k, pltpu.roll(carry, k, axis=0), r[:hist])
    return jnp.concatenate([top, r[hist:]], axis=0)


def _inproj_kernel(x_ref, nw_ref, w_ref, wdt_ref, proj_ref, dt_ref, h_ref):
    @pl.when(pl.program_id(1) == 0)
    def _():
        h = _rms(x_ref[...], nw_ref[...]).astype(BF16)
        h_ref[...] = h
        dt_ref[...] = _dot(h, wdt_ref[...])

    proj_ref[...] = _dot(h_ref[...], w_ref[...]).astype(BF16)


def _inproj(x2, nw, wp, wdt, tm=1024, tn=1024):
    t = x2.shape[0]
    return pl.pallas_call(
        _inproj_kernel,
        grid=(t // tm, PROJ_COLS // tn),
        in_specs=[
            pl.BlockSpec((tm, D_MODEL), lambda i, j: (i, 0)),
            pl.BlockSpec((1, D_MODEL), lambda i, j: (0, 0)),
            pl.BlockSpec((D_MODEL, tn), lambda i, j: (0, j)),
            pl.BlockSpec((D_MODEL, LANES), lambda i, j: (0, 0)),
        ],
        out_specs=[
            pl.BlockSpec((tm, tn), lambda i, j: (i, j)),
            pl.BlockSpec((tm, LANES), lambda i, j: (i, 0)),
        ],
        out_shape=[
            jax.ShapeDtypeStruct((t, PROJ_COLS), BF16),
            jax.ShapeDtypeStruct((t, LANES), F32),
        ],
        scratch_shapes=[pltpu.VMEM((tm, D_MODEL), BF16)],
        compiler_params=pltpu.CompilerParams(
            dimension_semantics=("arbitrary", "arbitrary"), vmem_limit_bytes=VMEM_LIMIT),
        name="inproj",
    )(x2, nw, wp, wdt)


def _vt_kernel(x_ref, nw_ref, wvt_ref, vt_ref):
    h = _rms(x_ref[...], nw_ref[...]).astype(BF16)
    vt = _dot_nt(wvt_ref[...], h)
    for c in range(vt_ref.shape[0]):
        vt_ref[c] = vt[:, c * MOBA_BLOCK:(c + 1) * MOBA_BLOCK].astype(BF16)


def _vproj_t(x2, nw, wvt, tm=512):
    t = x2.shape[0]
    return pl.pallas_call(
        _vt_kernel,
        grid=(t // tm,),
        in_specs=[
            pl.BlockSpec((tm, D_MODEL), lambda i: (i, 0)),
            pl.BlockSpec((1, D_MODEL), lambda i: (0, 0)),
            pl.BlockSpec((ATTN_WIDTH, D_MODEL), lambda i: (0, 0)),
        ],
        out_specs=pl.BlockSpec((tm // MOBA_BLOCK, ATTN_WIDTH, MOBA_BLOCK), lambda i: (i, 0, 0)),
        out_shape=jax.ShapeDtypeStruct((t // MOBA_BLOCK, ATTN_WIDTH, MOBA_BLOCK), BF16),
        compiler_params=pltpu.CompilerParams(
            dimension_semantics=("arbitrary",), vmem_limit_bytes=VMEM_LIMIT),
        name="vproj_t",
    )(x2, nw, wvt)


def _moba_kernel(*refs):
    for u in range(MOBA_TILES_PER_STEP):
        _moba_tile(u, *refs)


def _moba_tile(u, slopes_ref, q_ref, k_ref, vt_ref, o_ref, km_ref, kh_ref, vts_ref, sel_ref, *ring_refs):
    hp = pl.program_id(1)
    i = pl.program_id(2) * MOBA_TILES_PER_STEP + u
    nb = km_ref.shape[0]
    blk = MOBA_BLOCK
    rows_u = slice(u * blk, (u + 1) * blk)
    heads = range(HEADS_PER_LANE_TILE)
    den_rows = 16
    vt_rows = ATTN_HEAD_DIM + den_rows

    if u == 0:
        @pl.when(i == 0)
        def _():
            km_ref[...] = jnp.zeros_like(km_ref)

    q_raw = q_ref[rows_u, :]
    qs = (q_raw.astype(F32) * (ATTN_HEAD_DIM ** -0.5 * LOG2E)).astype(BF16)
    lane = lax.broadcasted_iota(jnp.int32, (1, LANES), 1)
    key_pos = lax.broadcasted_iota(jnp.int32, (blk, blk), 0)
    qry_pos = lax.broadcasted_iota(jnp.int32, (blk, blk), 1)
    key_lane = lax.broadcasted_iota(jnp.int32, (1, blk), 1)
    blk_id = lax.broadcasted_iota(jnp.int32, (nb, blk), 0)
    k_own = k_ref[rows_u, :]
    vt_own = vt_ref[u]
    valid = blk_id < i

    head_lanes, vrows, slope = [], [], []
    for hh in heads:
        lo_lane = hh * ATTN_HEAD_DIM
        head_lanes.append((lane >= lo_lane) & (lane < lo_lane + ATTN_HEAD_DIM))
        vrows.append(slice(lo_lane, lo_lane + ATTN_HEAD_DIM))
        slope.append(slopes_ref[hp * HEADS_PER_LANE_TILE + hh] * LOG2E)

    kh_own = jnp.concatenate([jnp.where(head_lanes[hh], k_own, jnp.zeros_like(k_own)) for hh in heads], axis=0)
    kh_ref[i] = kh_own

    km_ref[pl.ds(i, 1), :] = jnp.mean(k_own.astype(F32), axis=0, keepdims=True)
    km = km_ref[...]
    gate_lhs = jnp.concatenate(
        [part for hh in heads for part in _split3(jnp.where(head_lanes[hh], km, 0.0))], axis=0)
    gate_all = _dot_nt(gate_lhs, q_raw)
    for hh in heads:
        gate = sum(gate_all[(3 * hh + c) * nb:(3 * hh + c + 1) * nb] for c in range(3))
        g = jnp.where(valid, gate, -jnp.inf)
        sel = jnp.zeros((nb, blk), jnp.bool_)
        for _ in range(MOBA_TOPK):
            top = jnp.max(g, axis=0, keepdims=True)
            first = jnp.min(jnp.where(g == top, blk_id, nb), axis=0, keepdims=True)
            pick = blk_id == first
            sel = sel | (pick & (top > -jnp.inf))
            g = jnp.where(pick, -jnp.inf, g)
        sel_ref[hh, :nb, :] = sel.astype(F32)
        sel_ref[hh, nb:, :] = jnp.zeros((sel_ref.shape[1] - nb, blk), F32)

    def store_scores(j, dst_ref):
        dst_ref[...] = _dot_nt(kh_ref[jnp.minimum(j, i)], qs)

    row0 = lax.broadcasted_iota(jnp.int32, (den_rows, blk), 0) == 0
    own_slot = nb + u
    for hh in heads:
        w = jnp.exp2(slope[hh] * (key_lane - (blk - 1)).astype(F32))
        v_rows = slice(hh * vt_rows, hh * vt_rows + ATTN_HEAD_DIM)
        d_rows = slice(hh * vt_rows + ATTN_HEAD_DIM, (hh + 1) * vt_rows)
        vts_ref[i, v_rows, :] = (vt_own[vrows[hh], :].astype(F32) * w).astype(BF16)
        vts_ref[i, d_rows, :] = jnp.where(row0, w, 0.0).astype(BF16)
        vts_ref[own_slot, v_rows, :] = vt_own[vrows[hh], :]
        vts_ref[own_slot, d_rows, :] = jnp.where(row0, 1.0, 0.0).astype(BF16)

    def softmax_part(src_ref, dst_ref, j, carry):
        out = list(carry)
        for hh in heads:
            m = carry[4 * hh]
            s = src_ref[hh * blk:(hh + 1) * blk, :]
            c_j = slope[hh] * (blk * (j - i) + (blk - 1)).astype(F32)
            chosen = sel_ref[hh, pl.ds(j, 1), :] > 0.5
            m_new = jnp.where(chosen, jnp.maximum(m, jnp.max(s, axis=0, keepdims=True) + c_j), m)
            p = jnp.exp2(s - jnp.where(chosen, m_new - c_j, -NEG_BIG))
            dst_ref[hh * blk:(hh + 1) * blk, :] = p.astype(BF16)
            out[4 * hh] = m_new
            out[4 * hh + 3] = jnp.exp2(m - m_new)
        return tuple(out)

    def value_part(src_ref, j, carry):
        slot = jnp.where(j < 0, own_slot, jnp.minimum(j, i))
        out = list(carry)
        for hh in heads:
            _, l, acc, alpha = carry[4 * hh:4 * hh + 4]
            r = _dot(vts_ref[slot, hh * vt_rows:(hh + 1) * vt_rows, :],
                     src_ref[hh * blk:(hh + 1) * blk, :])
            out[4 * hh + 1] = alpha * l + r[ATTN_HEAD_DIM:ATTN_HEAD_DIM + 1]
            out[4 * hh + 2] = alpha * acc + r[:ATTN_HEAD_DIM]
        return tuple(out)

    ring = MOBA_RING
    sc = ring_refs[:ring]
    pb = ring_refs[ring:]
    store_scores(0, sc[0])
    store_scores(1, sc[1])

    s_own = _dot_nt(kh_own, qs)
    col_bias = key_pos.astype(F32)
    state = []
    for hh in heads:
        s = s_own[hh * blk:(hh + 1) * blk] + slope[hh] * col_bias
        s = jnp.where(key_pos <= qry_pos, s, NEG_BIG)
        m = jnp.max(s, axis=0, keepdims=True)
        pb[ring - 1][hh * blk:(hh + 1) * blk, :] = jnp.exp2(s - m).astype(BF16)
        state += [m, jnp.zeros_like(m), jnp.zeros((ATTN_HEAD_DIM, blk), F32), jnp.ones_like(m)]

    def body(t, carry):
        for c in range(ring):
            k = ring * t + c
            store_scores(k + 2, sc[(c + 2) % ring])
            carry = value_part(pb[(c - 1) % ring], k - 1, carry)
            carry = softmax_part(sc[c], pb[c], k, carry)
        return carry

    trips = (i + ring - 1) // ring
    state = lax.fori_loop(0, trips, body, tuple(state))
    state = value_part(pb[ring - 1], ring * trips - 1, state)
    outs = [state[4 * hh + 2] / state[4 * hh + 1] for hh in heads]
    o_ref[rows_u, :] = jnp.concatenate(outs, axis=0).T.astype(o_ref.dtype)


def _moba(slopes, proj, vt3, batch, seq):
    t = proj.shape[0]
    nq = seq // MOBA_BLOCK
    n_pairs = ATTN_N_HEADS // HEADS_PER_LANE_TILE
    q_blk0 = Q_OFF // LANES
    k_blk0 = K_OFF // LANES
    stacked = HEADS_PER_LANE_TILE * MOBA_BLOCK
    per_step = MOBA_TILES_PER_STEP
    ns = nq // per_step
    rows = per_step * MOBA_BLOCK
    return pl.pallas_call(
        _moba_kernel,
        grid=(batch, n_pairs, ns),
        in_specs=[
            pl.BlockSpec(memory_space=pltpu.SMEM),
            pl.BlockSpec((rows, LANES), lambda b, hp, i: (b * ns + i, q_blk0 + hp)),
            pl.BlockSpec((rows, LANES), lambda b, hp, i: (b * ns + i, k_blk0 + hp)),
            pl.BlockSpec((per_step, LANES, MOBA_BLOCK), lambda b, hp, i: (b * ns + i, hp, 0)),
        ],
        out_specs=pl.BlockSpec((rows, LANES), lambda b, hp, i: (b * ns + i, hp)),
        out_shape=jax.ShapeDtypeStruct((t, ATTN_WIDTH), BF16),
        scratch_shapes=[
            pltpu.VMEM((nq, LANES), F32),
            pltpu.VMEM((nq, stacked, LANES), BF16),
            pltpu.VMEM((nq + per_step, HEADS_PER_LANE_TILE * (ATTN_HEAD_DIM + 16), MOBA_BLOCK), BF16),
            pltpu.VMEM((HEADS_PER_LANE_TILE, nq + 8, MOBA_BLOCK), F32),
        ] + [pltpu.VMEM((stacked, MOBA_BLOCK), F32)] * MOBA_RING + [pltpu.VMEM((stacked, MOBA_BLOCK), BF16)] * MOBA_RING,
        compiler_params=pltpu.CompilerParams(
            dimension_semantics=("arbitrary", "arbitrary", "arbitrary"), vmem_limit_bytes=VMEM_LIMIT),
        name="moba",
    )(slopes, proj, proj, vt3)


def _conv_silu(x, carry, w, b):
    y = w[3:4, :] * x + b
    for k in range(1, SSD_CONV):
        y = y + w[SSD_CONV - 1 - k:SSD_CONV - k, :] * _shift_rows(x, k, carry)
    return _silu(y)


def _ssd_kernel(z_ref, xs_ref, b_ref, c_ref, dt_ref, cw_ref, cb_ref, dtb_ref, alog_ref, dsk_ref, nw_ref,
                e_ref, y_ref, st_ref, cx_ref, cbc_ref):
    L = SSD_CHUNK

    @pl.when(pl.program_id(1) == 0)
    def _():
        st_ref[...] = jnp.zeros_like(st_ref)
        cx_ref[...] = jnp.zeros_like(cx_ref)
        cbc_ref[...] = jnp.zeros_like(cbc_ref)

    def conv_silu(src_ref, hist_ref, cols, hist_cols, w_cols):
        raw = src_ref[:, cols].astype(F32)
        out = _conv_silu(raw, hist_ref[:, hist_cols], cw_ref[:, w_cols], cb_ref[:, w_cols])
        hist_ref[:, hist_cols] = raw[L - 8:, :]
        return out

    xdt_in = dt_ref[...] + dtb_ref[...]
    dt = jnp.maximum(xdt_in, 0.0) + jnp.log1p(jnp.exp(-jnp.abs(xdt_in)))
    adt = dt * (-jnp.exp(alog_ref[...]))
    row = lax.broadcasted_iota(jnp.int32, (L, L), 0)
    col = lax.broadcasted_iota(jnp.int32, (L, L), 1)
    causal = row >= col
    tri = causal.astype(BF16)
    a_hi, a_mid, a_lo = _split3(adt)
    acs = _dot(tri, a_hi) + _dot(tri, a_mid) + _dot(tri, a_lo)
    acs_t = acs.T
    eacs = jnp.exp(acs)
    wend = jnp.exp(acs[L - 1:L, :] - acs)

    stacked = jnp.concatenate([dt, eacs, wend], axis=0)
    s_parts = _split3(stacked)
    lane = lax.broadcasted_iota(jnp.int32, (L, LANES), 1)
    first_head = lane < SSD_HEAD_DIM

    for g in range(SSD_N_GROUPS):
        gl = slice(g * SSD_GROUP_W, (g + 1) * SSD_GROUP_W)
        nl = slice(g * SSD_D_STATE, (g + 1) * SSD_D_STATE)
        cl = slice(SSD_GN + g * SSD_D_STATE, SSD_GN + (g + 1) * SSD_D_STATE)
        e = e_ref[:, gl]
        expanded = sum(_dot(part, e) for part in s_parts)
        dt_x = expanded[:L]
        eacs_x = expanded[L:2 * L]
        wend_x = expanded[2 * L:]

        xs = conv_silu(xs_ref, cx_ref, gl, gl, gl)
        bg = conv_silu(b_ref, cbc_ref, nl, nl, slice(SSD_D_INNER + nl.start, SSD_D_INNER + nl.stop))
        cg = conv_silu(c_ref, cbc_ref, nl, cl, slice(SSD_D_INNER + cl.start, SSD_D_INNER + cl.stop)).astype(BF16)
        xdt = xs * dt_x
        xdt_b = xdt.astype(BF16)
        xw_b = (xdt * wend_x).astype(BF16)

        cb = _dot_nt(cg, bg.astype(BF16))
        y_inter = _dot(cg, st_ref[:, gl].astype(BF16))
        pairs = []
        for pr in range(SSD_GROUP_W // LANES):
            h0 = (g * SSD_GROUP_W + pr * LANES) // SSD_HEAD_DIM
            xp = xdt_b[:, pr * LANES:(pr + 1) * LANES]
            ys = []
            for h in (h0, h0 + 1):
                seg = acs[:, h:h + 1] - acs_t[h:h + 1, :]
                decay = jnp.exp(jnp.where(causal, seg, NEG_BIG))
                ys.append(_dot((cb * decay).astype(BF16), xp))
            pairs.append(jnp.where(first_head, ys[0], ys[1]))
        y = jnp.concatenate(pairs, axis=1) + y_inter * eacs_x + dsk_ref[:, gl] * xs
        st_ref[:, gl] = st_ref[:, gl] * eacs_x[L - 1:L, :] + _dot(bg.T.astype(BF16), xw_b)
        y = y * _silu(z_ref[:, gl].astype(F32))
        y_ref[:, gl] = _rms(y, nw_ref[:, gl]).astype(y_ref.dtype)


def _ssd(proj, dt_raw, cw, cb, dtb, alog, dsk, nw, expand, batch, seq):
    t = proj.shape[0]
    nc = seq // SSD_CHUNK
    L = SSD_CHUNK
    conv_dim = SSD_D_INNER + 2 * SSD_GN

    def rows(b, c):
        return b * nc + c

    full = lambda shape: pl.BlockSpec(shape, lambda b, c: (0, 0))
    return pl.pallas_call(
        _ssd_kernel,
        grid=(batch, nc),
        in_specs=[
            pl.BlockSpec((L, SSD_D_INNER), lambda b, c: (rows(b, c), Z_OFF // SSD_D_INNER)),
            pl.BlockSpec((L, SSD_D_INNER), lambda b, c: (rows(b, c), XS_OFF // SSD_D_INNER)),
            pl.BlockSpec((L, SSD_GN), lambda b, c: (rows(b, c), B_OFF // SSD_GN)),
            pl.BlockSpec((L, SSD_GN), lambda b, c: (rows(b, c), C_OFF // SSD_GN)),
            pl.BlockSpec((L, LANES), lambda b, c: (rows(b, c), 0)),
            full((SSD_CONV, conv_dim)),
            full((1, conv_dim)),
            full((1, LANES)),
            full((1, LANES)),
            full((1, SSD_D_INNER)),
            full((1, SSD_D_INNER)),
            full((LANES, SSD_D_INNER)),
        ],
        out_specs=pl.BlockSpec((L, SSD_D_INNER), lambda b, c: (rows(b, c), 0)),
        out_shape=jax.ShapeDtypeStruct((t, SSD_D_INNER), BF16),
        scratch_shapes=[
            pltpu.VMEM((SSD_D_STATE, SSD_D_INNER), F32),
            pltpu.VMEM((8, SSD_D_INNER), F32),
            pltpu.VMEM((8, 2 * SSD_GN), F32),
        ],
        compiler_params=pltpu.CompilerParams(
            dimension_semantics=("arbitrary", "arbitrary"), vmem_limit_bytes=VMEM_LIMIT),
        name="ssd",
    )(proj, proj, proj, proj, dt_raw, cw, cb, dtb, alog, dsk, nw, expand)


def _mix_kernel(yn_ref, att_ref, gs_ref, ga_ref, x_ref, wssd_ref, wattn_ref, wout_ref, nw_ref, o_ref):
    y_ssd = _dot(yn_ref[...], wssd_ref[...])
    y_attn = _dot(att_ref[...], wattn_ref[...])
    mixed = _sigmoid(gs_ref[...].astype(F32)) * y_ssd + _sigmoid(ga_ref[...].astype(F32)) * y_attn
    o = _dot(mixed.astype(BF16), wout_ref[...])
    o_ref[...] = x_ref[...] + _rms(o, nw_ref[...])


def _mix(ynorm, att, proj, x2, wssd, wattn, wout, nw, tm=512):
    t = x2.shape[0]
    full = lambda shape: pl.BlockSpec(shape, lambda i: (0, 0))
    return pl.pallas_call(
        _mix_kernel,
        grid=(t // tm,),
        in_specs=[
            pl.BlockSpec((tm, SSD_D_INNER), lambda i: (i, 0)),
            pl.BlockSpec((tm, ATTN_WIDTH), lambda i: (i, 0)),
            pl.BlockSpec((tm, D_MODEL), lambda i: (i, GS_OFF // D_MODEL)),
            pl.BlockSpec((tm, D_MODEL), lambda i: (i, GA_OFF // D_MODEL)),
            pl.BlockSpec((tm, D_MODEL), lambda i: (i, 0)),
            full((SSD_D_INNER, D_MODEL)),
            full((ATTN_WIDTH, D_MODEL)),
            full((D_MODEL, D_MODEL)),
            full((1, D_MODEL)),
        ],
        out_specs=pl.BlockSpec((tm, D_MODEL), lambda i: (i, 0)),
        out_shape=jax.ShapeDtypeStruct((t, D_MODEL), F32),
        compiler_params=pltpu.CompilerParams(
            dimension_semantics=("arbitrary",), vmem_limit_bytes=VMEM_LIMIT),
        name="mix",
    )(ynorm, att, proj, proj, x2, wssd, wattn, wout, nw)


def _ffn_kernel(x_ref, nw1_ref, wg_ref, wu_ref, cw_ref, cb_ref, wd_ref, nw2_ref, o_ref,
                h_ref, acc_ref, carry_ref, *, tiles_per_seq):
    i = pl.program_id(0)
    k = pl.program_id(1)
    tm = x_ref.shape[0]

    @pl.when((i == 0) & (k == 0))
    def _():
        carry_ref[...] = jnp.zeros_like(carry_ref)

    @pl.when(k == 0)
    def _():
        h_ref[...] = _rms(x_ref[...], nw1_ref[...]).astype(BF16)
        acc_ref[...] = jnp.zeros_like(acc_ref)

    h = h_ref[...]
    gate = _dot(h, wg_ref[...])
    up = _dot(h, wu_ref[...])
    seq_start = (i % tiles_per_seq) == 0
    carry = jnp.where(seq_start, 0.0, carry_ref[k])
    carry_ref[k] = gate[tm - 8:, :]
    cw = cw_ref[...]
    conv = cw[FFN_CONV - 1:FFN_CONV, :] * gate + cb_ref[...]
    for s in range(1, FFN_CONV):
        conv = conv + cw[FFN_CONV - 1 - s:FFN_CONV - s, :] * _shift_rows(gate, s, carry)
    c0 = 0.7978845608028654
    gelu = 0.5 * conv * (1.0 + jnp.tanh(c0 * (conv + 0.044715 * (conv * conv * conv))))
    acc_ref[...] += _dot((gelu * up).astype(BF16), wd_ref[...])

    @pl.when(k == pl.num_programs(1) - 1)
    def _():
        o_ref[...] = x_ref[...] + _rms(acc_ref[...], nw2_ref[...])


def _ffn(x1, nw1, wup, cw, cb, wd, nw2, seq, tm=1024, tk=512):
    t = x1.shape[0]
    nk = FFN_HIDDEN // tk
    kern = functools.partial(_ffn_kernel, tiles_per_seq=seq // tm)
    return pl.pallas_call(
        kern,
        grid=(t // tm, nk),
        in_specs=[
            pl.BlockSpec((tm, D_MODEL), lambda i, k: (i, 0)),
            pl.BlockSpec((1, D_MODEL), lambda i, k: (0, 0)),
            pl.BlockSpec((D_MODEL, tk), lambda i, k: (0, k)),
            pl.BlockSpec((D_MODEL, tk), lambda i, k: (0, nk + k)),
            pl.BlockSpec((FFN_CONV, tk), lambda i, k: (0, k)),
            pl.BlockSpec((1, tk), lambda i, k: (0, k)),
            pl.BlockSpec((tk, D_MODEL), lambda i, k: (k, 0)),
            pl.BlockSpec((1, D_MODEL), lambda i, k: (0, 0)),
        ],
        out_specs=pl.BlockSpec((tm, D_MODEL), lambda i, k: (i, 0)),
        out_shape=jax.ShapeDtypeStruct((t, D_MODEL), F32),
        scratch_shapes=[
            pltpu.VMEM((tm, D_MODEL), BF16),
            pltpu.VMEM((tm, D_MODEL), F32),
            pltpu.VMEM((nk, 8, tk), F32),
        ],
        compiler_params=pltpu.CompilerParams(
            dimension_semantics=("arbitrary", "arbitrary"), vmem_limit_bytes=VMEM_LIMIT),
        name="ffn",
    )(x1, nw1, wup, wup, cw, cb, wd, nw2)


def _pad_lanes(v):
    return jnp.pad(v.astype(F32), (0, LANES - v.shape[0]))[None, :]


def kernel(x, pre_mix_norm, w_in, ssd_conv_w, ssd_conv_b, ssd_dt_bias, ssd_a_log, ssd_d_skip, ssd_out_norm,
           w_ssd_branch, w_attn_branch, w_out, post_mix_norm, pre_ffn_norm, w_ffn_up, ffn_conv_w, ffn_conv_b,
           w_ffn_down, post_ffn_norm):
    batch, seq, d = x.shape
    assert d == D_MODEL and seq % 1024 == 0
    depth = w_in.shape[0]
    x2 = x.reshape(batch * seq, d)

    slopes = 2.0 ** (-8.0 * jnp.arange(1, ATTN_N_HEADS + 1, dtype=F32) / ATTN_N_HEADS)
    head_of_channel = jnp.arange(SSD_D_INNER) // SSD_HEAD_DIM
    expand = (jnp.arange(LANES)[:, None] == head_of_channel[None, :]).astype(BF16)
    st = _IN_STARTS

    for li in range(depth):
        w = w_in[li]
        wp = jnp.concatenate([w[:, st[0]:st[2]], w[:, st[3]:st[5]], w[:, st[6]:]], axis=1).astype(BF16)
        wdt = jnp.pad(w[:, st[2]:st[3]], ((0, 0), (0, LANES - SSD_N_HEADS))).astype(BF16)
        wvt = w[:, st[5]:st[6]].T.astype(BF16)
        nw_mix = pre_mix_norm[li][None, :]

        proj, dt_raw = _inproj(x2, nw_mix, wp, wdt)
        vt3 = _vproj_t(x2, nw_mix, wvt)
        att = _moba(slopes, proj, vt3, batch, seq)
        ynorm = _ssd(proj, dt_raw, ssd_conv_w[li], ssd_conv_b[li][None, :], _pad_lanes(ssd_dt_bias[li]),
                     _pad_lanes(ssd_a_log[li]), jnp.repeat(ssd_d_skip[li].astype(F32), SSD_HEAD_DIM)[None, :],
                     ssd_out_norm[li][None, :], expand, batch, seq)
        x1 = _mix(ynorm, att, proj, x2, w_ssd_branch[li].astype(BF16), w_attn_branch[li].astype(BF16),
                  w_out[li].astype(BF16), post_mix_norm[li][None, :])
        x2 = _ffn(x1, pre_ffn_norm[li][None, :], w_ffn_up[li].astype(BF16), ffn_conv_w[li],
                  ffn_conv_b[li][None, :], w_ffn_down[li].astype(BF16), post_ffn_norm[li][None, :], seq)
    return x2.reshape(batch, seq, d)
```

```python
import functools

import jax
import jax.numpy as jnp
from jax import lax
from jax.experimental import pallas as pl
from jax.experimental.pallas import tpu as pltpu

F32 = jnp.float32
BF16 = jnp.bfloat16

D_MODEL = 1024
SSD_D_INNER = 2048
SSD_HEAD_DIM = 64
SSD_N_HEADS = 32
SSD_N_GROUPS = 4
SSD_D_STATE = 128
SSD_CONV = 4
SSD_CHUNK = 128
SSD_GN = SSD_N_GROUPS * SSD_D_STATE
SSD_GROUP_W = SSD_D_INNER // SSD_N_GROUPS
ATTN_HEAD_DIM = 64
ATTN_N_HEADS = 16
ATTN_WIDTH = 1024
MOBA_BLOCK = 256
MOBA_TOPK = 3
MOBA_RING = 6
MOBA_AHEAD = 2
MOBA_TILES_PER_STEP = 4
FFN_HIDDEN = 4096
FFN_CONV = 3
NORM_EPS = 1e-6

LANES = 128
HEADS_PER_LANE_TILE = LANES // ATTN_HEAD_DIM
NEG_BIG = -1e30
LOG2E = 1.4426950408889634

_IN_SIZES = (SSD_D_INNER, SSD_D_INNER + 2 * SSD_GN, SSD_N_HEADS, ATTN_WIDTH, ATTN_WIDTH, ATTN_WIDTH,
             D_MODEL, D_MODEL)
_IN_STARTS = tuple(sum(_IN_SIZES[:j]) for j in range(len(_IN_SIZES)))
Z_OFF = 0
XS_OFF = 2048
B_OFF = 4096
C_OFF = 4608
Q_OFF = 5120
K_OFF = 6144
GS_OFF = 7168
GA_OFF = 8192
PROJ_COLS = 9216

VMEM_LIMIT = 48 * 1024 * 1024


def _dot(a, b):
    return jnp.dot(a, b, preferred_element_type=F32)


def _dot_nt(a, b):
    return lax.dot_general(a, b, (((1,), (1,)), ((), ())), preferred_element_type=F32)


def _rms(x, w):
    return x * lax.rsqrt(jnp.mean(x * x, axis=-1, keepdims=True) + NORM_EPS) * w


def _sigmoid(x):
    return 1.0 / (1.0 + jnp.exp(-x))


def _silu(x):
    return x * _sigmoid(x)


def _split3(v):
    hi = v.astype(BF16)
    r1 = v - hi.astype(F32)
    mid = r1.astype(BF16)
    lo = (r1 - mid.astype(F32)).astype(BF16)
    return hi, mid, lo


def _shift_rows(x, k, carry):
    r = pltpu.roll(x, k, axis=0)
    hist = carry.shape[0]
    row = lax.broadcasted_iota(jnp.int32, carry.shape, 0)
    top = jnp.where(row < k, pltpu.roll(carry, k, axis=0), r[:hist])
    return jnp.concatenate([top, r[hist:]], axis=0)


def _inproj_kernel(x_ref, nw_ref, w_ref, wdt_ref, proj_ref, dt_ref, h_ref):
    @pl.when(pl.program_id(1) == 0)
    def _():
        h = _rms(x_ref[...], nw_ref[...]).astype(BF16)
        h_ref[...] = h
        dt_ref[...] = _dot(h, wdt_ref[...])

    proj_ref[...] = _dot(h_ref[...], w_ref[...]).astype(BF16)


def _inproj(x2, nw, wp, wdt, tm=1024, tn=1536):
    t = x2.shape[0]
    return pl.pallas_call(
        _inproj_kernel,
        grid=(t // tm, PROJ_COLS // tn),
        in_specs=[
            pl.BlockSpec((tm, D_MODEL), lambda i, j: (i, 0)),
            pl.BlockSpec((1, D_MODEL), lambda i, j: (0, 0)),
            pl.BlockSpec((D_MODEL, tn), lambda i, j: (0, j)),
            pl.BlockSpec((D_MODEL, LANES), lambda i, j: (0, 0)),
        ],
        out_specs=[
            pl.BlockSpec((tm, tn), lambda i, j: (i, j)),
            pl.BlockSpec((tm, LANES), lambda i, j: (i, 0)),
        ],
        out_shape=[
            jax.ShapeDtypeStruct((t, PROJ_COLS), BF16),
            jax.ShapeDtypeStruct((t, LANES), F32),
        ],
        scratch_shapes=[pltpu.VMEM((tm, D_MODEL), BF16)],
        compiler_params=pltpu.CompilerParams(
            dimension_semantics=("arbitrary", "arbitrary"), vmem_limit_bytes=VMEM_LIMIT),
        name="inproj",
    )(x2, nw, wp, wdt)


def _vt_kernel(x_ref, nw_ref, wvt_ref, vt_ref):
    h = _rms(x_ref[...], nw_ref[...]).astype(BF16)
    vt = _dot_nt(wvt_ref[...], h)
    for c in range(vt_ref.shape[0]):
        vt_ref[c] = vt[:, c * MOBA_BLOCK:(c + 1) * MOBA_BLOCK].astype(BF16)


def _vproj_t(x2, nw, wvt, tm=512):
    t = x2.shape[0]
    return pl.pallas_call(
        _vt_kernel,
        grid=(t // tm,),
        in_specs=[
            pl.BlockSpec((tm, D_MODEL), lambda i: (i, 0)),
            pl.BlockSpec((1, D_MODEL), lambda i: (0, 0)),
            pl.BlockSpec((ATTN_WIDTH, D_MODEL), lambda i: (0, 0)),
        ],
        out_specs=pl.BlockSpec((tm // MOBA_BLOCK, ATTN_WIDTH, MOBA_BLOCK), lambda i: (i, 0, 0)),
        out_shape=jax.ShapeDtypeStruct((t // MOBA_BLOCK, ATTN_WIDTH, MOBA_BLOCK), BF16),
        compiler_params=pltpu.CompilerParams(
            dimension_semantics=("arbitrary",), vmem_limit_bytes=VMEM_LIMIT),
        name="vproj_t",
    )(x2, nw, wvt)


def _moba_kernel(*refs):
    for u in range(MOBA_TILES_PER_STEP):
        _moba_tile(u, *refs)


def _moba_tile(u, slopes_ref, q_ref, k_ref, vt_ref, o_ref, km_ref, kh_ref, vts_ref, sel_ref, *ring_refs):
    hp = pl.program_id(1)
    i = pl.program_id(2) * MOBA_TILES_PER_STEP + u
    nb = km_ref.shape[0]
    blk = MOBA_BLOCK
    rows_u = slice(u * blk, (u + 1) * blk)
    heads = range(HEADS_PER_LANE_TILE)
    den_rows = 16
    vt_rows = ATTN_HEAD_DIM + den_rows

    if u == 0:
        @pl.when(i == 0)
        def _():
            km_ref[...] = jnp.zeros_like(km_ref)

    q_raw = q_ref[rows_u, :]
    qs = (q_raw.astype(F32) * (ATTN_HEAD_DIM ** -0.5 * LOG2E)).astype(BF16)
    lane = lax.broadcasted_iota(jnp.int32, (1, LANES), 1)
    key_pos = lax.broadcasted_iota(jnp.int32, (blk, blk), 0)
    qry_pos = lax.broadcasted_iota(jnp.int32, (blk, blk), 1)
    key_lane = lax.broadcasted_iota(jnp.int32, (1, blk), 1)
    blk_id = lax.broadcasted_iota(jnp.int32, (nb, blk), 0)
    k_own = k_ref[rows_u, :]
    vt_own = vt_ref[u]
    valid = blk_id < i

    head_lanes, vrows, slope = [], [], []
    for hh in heads:
        lo_lane = hh * ATTN_HEAD_DIM
        head_lanes.append((lane >= lo_lane) & (lane < lo_lane + ATTN_HEAD_DIM))
        vrows.append(slice(lo_lane, lo_lane + ATTN_HEAD_DIM))
        slope.append(slopes_ref[hp * HEADS_PER_LANE_TILE + hh] * LOG2E)

    kh_own = jnp.concatenate([jnp.where(head_lanes[hh], k_own, jnp.zeros_like(k_own)) for hh in heads], axis=0)
    kh_ref[i] = kh_own

    km_ref[pl.ds(i, 1), :] = jnp.mean(k_own.astype(F32), axis=0, keepdims=True)
    km = km_ref[...]
    gate_lhs = jnp.concatenate(
        [part for hh in heads for part in _split3(jnp.where(head_lanes[hh], km, 0.0))], axis=0)
    gate_all = _dot_nt(gate_lhs, q_raw)
    for hh in heads:
        gate = sum(gate_all[(3 * hh + c) * nb:(3 * hh + c + 1) * nb] for c in range(3))
        g = jnp.where(valid, gate, -jnp.inf)
        sel = jnp.zeros((nb, blk), jnp.bool_)
        for _ in range(MOBA_TOPK):
            top = jnp.max(g, axis=0, keepdims=True)
            first = jnp.min(jnp.where(g == top, blk_id, nb), axis=0, keepdims=True)
            pick = blk_id == first
            sel = sel | (pick & (top > -jnp.inf))
            g = jnp.where(pick, -jnp.inf, g)
        sel_ref[hh, :nb, :] = sel.astype(F32)
        sel_ref[hh, nb:, :] = jnp.zeros((sel_ref.shape[1] - nb, blk), F32)

    def store_scores(j, dst_ref):
        dst_ref[...] = _dot_nt(kh_ref[jnp.minimum(j, i)], qs)

    row0 = lax.broadcasted_iota(jnp.int32, (den_rows, blk), 0) == 0
    own_slot = nb + u
    for hh in heads:
        w = jnp.exp2(slope[hh] * (key_lane - (blk - 1)).astype(F32))
        v_rows = slice(hh * vt_rows, hh * vt_rows + ATTN_HEAD_DIM)
        d_rows = slice(hh * vt_rows + ATTN_HEAD_DIM, (hh + 1) * vt_rows)
        vts_ref[i, v_rows, :] = (vt_own[vrows[hh], :].astype(F32) * w).astype(BF16)
        vts_ref[i, d_rows, :] = jnp.where(row0, w, 0.0).astype(BF16)
        vts_ref[own_slot, v_rows, :] = vt_own[vrows[hh], :]
        vts_ref[own_slot, d_rows, :] = jnp.where(row0, 1.0, 0.0).astype(BF16)

    def softmax_part(src_ref, dst_ref, j, carry):
        out = list(carry)
        for hh in heads:
            m = carry[4 * hh]
            s = src_ref[hh * blk:(hh + 1) * blk, :]
            c_j = slope[hh] * (blk * (j - i) + (blk - 1)).astype(F32)
            chosen = sel_ref[hh, pl.ds(j, 1), :] > 0.5
            m_new = jnp.where(chosen, jnp.maximum(m, jnp.max(s, axis=0, keepdims=True) + c_j), m)
            p = jnp.exp2(s - jnp.where(chosen, m_new - c_j, -NEG_BIG))
            dst_ref[hh * blk:(hh + 1) * blk, :] = p.astype(BF16)
            out[4 * hh] = m_new
            out[4 * hh + 3] = jnp.exp2(m - m_new)
        return tuple(out)

    def value_part(src_ref, j, carry):
        slot = jnp.where(j < 0, own_slot, jnp.minimum(j, i))
        out = list(carry)
        for hh in heads:
            _, l, acc, alpha = carry[4 * hh:4 * hh + 4]
            r = _dot(vts_ref[slot, hh * vt_rows:(hh + 1) * vt_rows, :],
                     src_ref[hh * blk:(hh + 1) * blk, :])
            out[4 * hh + 1] = alpha * l + r[ATTN_HEAD_DIM:ATTN_HEAD_DIM + 1]
            out[4 * hh + 2] = alpha * acc + r[:ATTN_HEAD_DIM]
        return tuple(out)

    ring = MOBA_RING
    sc = ring_refs[:ring]
    pb = ring_refs[ring:]
    for c in range(MOBA_AHEAD):
        store_scores(c, sc[c])

    s_own = _dot_nt(kh_own, qs)
    col_bias = key_pos.astype(F32)
    state = []
    for hh in heads:
        s = s_own[hh * blk:(hh + 1) * blk] + slope[hh] * col_bias
        s = jnp.where(key_pos <= qry_pos, s, NEG_BIG)
        m = jnp.max(s, axis=0, keepdims=True)
        pb[ring - 1][hh * blk:(hh + 1) * blk, :] = jnp.exp2(s - m).astype(BF16)
        state += [m, jnp.zeros_like(m), jnp.zeros((ATTN_HEAD_DIM, blk), F32), jnp.ones_like(m)]

    def body(t, carry):
        for c in range(ring):
            k = ring * t + c
            store_scores(k + MOBA_AHEAD, sc[(c + MOBA_AHEAD) % ring])
            carry = value_part(pb[(c - 1) % ring], k - 1, carry)
            carry = softmax_part(sc[c], pb[c], k, carry)
        return carry

    trips = (i + ring - 1) // ring
    state = lax.fori_loop(0, trips, body, tuple(state))
    state = value_part(pb[ring - 1], ring * trips - 1, state)
    outs = [state[4 * hh + 2] / state[4 * hh + 1] for hh in heads]
    o_ref[rows_u, :] = jnp.concatenate(outs, axis=0).T.astype(o_ref.dtype)


def _moba(slopes, proj, vt3, batch, seq):
    t = proj.shape[0]
    nq = seq // MOBA_BLOCK
    n_pairs = ATTN_N_HEADS // HEADS_PER_LANE_TILE
    q_blk0 = Q_OFF // LANES
    k_blk0 = K_OFF // LANES
    stacked = HEADS_PER_LANE_TILE * MOBA_BLOCK
    per_step = MOBA_TILES_PER_STEP
    ns = nq // per_step
    rows = per_step * MOBA_BLOCK
    return pl.pallas_call(
        _moba_kernel,
        grid=(batch, n_pairs, ns),
        in_specs=[
            pl.BlockSpec(memory_space=pltpu.SMEM),
            pl.BlockSpec((rows, LANES), lambda b, hp, i: (b * ns + i, q_blk0 + hp)),
            pl.BlockSpec((rows, LANES), lambda b, hp, i: (b * ns + i, k_blk0 + hp)),
            pl.BlockSpec((per_step, LANES, MOBA_BLOCK), lambda b, hp, i: (b * ns + i, hp, 0)),
        ],
        out_specs=pl.BlockSpec((rows, LANES), lambda b, hp, i: (b * ns + i, hp)),
        out_shape=jax.ShapeDtypeStruct((t, ATTN_WIDTH), BF16),
        scratch_shapes=[
            pltpu.VMEM((nq, LANES), F32),
            pltpu.VMEM((nq, stacked, LANES), BF16),
            pltpu.VMEM((nq + per_step, HEADS_PER_LANE_TILE * (ATTN_HEAD_DIM + 16), MOBA_BLOCK), BF16),
            pltpu.VMEM((HEADS_PER_LANE_TILE, nq + 8, MOBA_BLOCK), F32),
        ] + [pltpu.VMEM((stacked, MOBA_BLOCK), F32)] * MOBA_RING + [pltpu.VMEM((stacked, MOBA_BLOCK), BF16)] * MOBA_RING,
        compiler_params=pltpu.CompilerParams(
            dimension_semantics=("arbitrary", "arbitrary", "arbitrary"), vmem_limit_bytes=VMEM_LIMIT),
        name="moba",
    )(slopes, proj, proj, vt3)


def _conv_silu(x, carry, w, b):
    y = w[3:4, :] * x + b
    for k in range(1, SSD_CONV):
        y = y + w[SSD_CONV - 1 - k:SSD_CONV - k, :] * _shift_rows(x, k, carry)
    return _silu(y)


def _ssd_kernel(z_ref, xs_ref, b_ref, c_ref, dt_ref, cw_ref, cb_ref, dtb_ref, alog_ref, dsk_ref, nw_ref,
                e_ref, y_ref, st_ref, cx_ref, cbc_ref):
    L = SSD_CHUNK

    @pl.when(pl.program_id(1) == 0)
    def _():
        st_ref[...] = jnp.zeros_like(st_ref)
        cx_ref[...] = jnp.zeros_like(cx_ref)
        cbc_ref[...] = jnp.zeros_like(cbc_ref)

    def conv_silu(src_ref, hist_ref, cols, hist_cols, w_cols):
        raw = src_ref[:, cols].astype(F32)
        out = _conv_silu(raw, hist_ref[:, hist_cols], cw_ref[:, w_cols], cb_ref[:, w_cols])
        hist_ref[:, hist_cols] = raw[L - 8:, :]
        return out

    xdt_in = dt_ref[...] + dtb_ref[...]
    dt = jnp.maximum(xdt_in, 0.0) + jnp.log1p(jnp.exp(-jnp.abs(xdt_in)))
    adt = dt * (-jnp.exp(alog_ref[...]))
    row = lax.broadcasted_iota(jnp.int32, (L, L), 0)
    col = lax.broadcasted_iota(jnp.int32, (L, L), 1)
    causal = row >= col
    tri = causal.astype(BF16)
    a_hi, a_mid, a_lo = _split3(adt)
    acs = _dot(tri, a_hi) + _dot(tri, a_mid) + _dot(tri, a_lo)
    acs_t = acs.T
    eacs = jnp.exp(acs)
    wend = jnp.exp(acs[L - 1:L, :] - acs)

    stacked = jnp.concatenate([dt, eacs, wend], axis=0)
    s_parts = _split3(stacked)
    lane = lax.broadcasted_iota(jnp.int32, (L, LANES), 1)
    first_head = lane < SSD_HEAD_DIM

    for g in range(SSD_N_GROUPS):
        gl = slice(g * SSD_GROUP_W, (g + 1) * SSD_GROUP_W)
        nl = slice(g * SSD_D_STATE, (g + 1) * SSD_D_STATE)
        cl = slice(SSD_GN + g * SSD_D_STATE, SSD_GN + (g + 1) * SSD_D_STATE)
        e = e_ref[:, gl]
        expanded = sum(_dot(part, e) for part in s_parts)
        dt_x = expanded[:L]
        eacs_x = expanded[L:2 * L]
        wend_x = expanded[2 * L:]

        xs = conv_silu(xs_ref, cx_ref, gl, gl, gl)
        bg = conv_silu(b_ref, cbc_ref, nl, nl, slice(SSD_D_INNER + nl.start, SSD_D_INNER + nl.stop))
        cg = conv_silu(c_ref, cbc_ref, nl, cl, slice(SSD_D_INNER + cl.start, SSD_D_INNER + cl.stop)).astype(BF16)
        xdt = xs * dt_x
        xdt_b = xdt.astype(BF16)
        xw_b = (xdt * wend_x).astype(BF16)

        cb = _dot_nt(cg, bg.astype(BF16))
        y_inter = _dot(cg, st_ref[:, gl].astype(BF16))
        pairs = []
        for pr in range(SSD_GROUP_W // LANES):
            h0 = (g * SSD_GROUP_W + pr * LANES) // SSD_HEAD_DIM
            xp = xdt_b[:, pr * LANES:(pr + 1) * LANES]
            ys = []
            for h in (h0, h0 + 1):
                seg = acs[:, h:h + 1] - acs_t[h:h + 1, :]
                decay = jnp.exp(jnp.where(causal, seg, NEG_BIG))
                ys.append(_dot((cb * decay).astype(BF16), xp))
            pairs.append(jnp.where(first_head, ys[0], ys[1]))
        y = jnp.concatenate(pairs, axis=1) + y_inter * eacs_x + dsk_ref[:, gl] * xs
        st_ref[:, gl] = st_ref[:, gl] * eacs_x[L - 1:L, :] + _dot(bg.T.astype(BF16), xw_b)
        y = y * _silu(z_ref[:, gl].astype(F32))
        y_ref[:, gl] = _rms(y, nw_ref[:, gl]).astype(y_ref.dtype)


def _ssd(proj, dt_raw, cw, cb, dtb, alog, dsk, nw, expand, batch, seq):
    t = proj.shape[0]
    nc = seq // SSD_CHUNK
    L = SSD_CHUNK
    conv_dim = SSD_D_INNER + 2 * SSD_GN

    def rows(b, c):
        return b * nc + c

    full = lambda shape: pl.BlockSpec(shape, lambda b, c: (0, 0))
    return pl.pallas_call(
        _ssd_kernel,
        grid=(batch, nc),
        in_specs=[
            pl.BlockSpec((L, SSD_D_INNER), lambda b, c: (rows(b, c), Z_OFF // SSD_D_INNER)),
            pl.BlockSpec((L, SSD_D_INNER), lambda b, c: (rows(b, c), XS_OFF // SSD_D_INNER)),
            pl.BlockSpec((L, SSD_GN), lambda b, c: (rows(b, c), B_OFF // SSD_GN)),
            pl.BlockSpec((L, SSD_GN), lambda b, c: (rows(b, c), C_OFF // SSD_GN)),
            pl.BlockSpec((L, LANES), lambda b, c: (rows(b, c), 0)),
            full((SSD_CONV, conv_dim)),
            full((1, conv_dim)),
            full((1, LANES)),
            full((1, LANES)),
            full((1, SSD_D_INNER)),
            full((1, SSD_D_INNER)),
            full((LANES, SSD_D_INNER)),
        ],
        out_specs=pl.BlockSpec((L, SSD_D_INNER), lambda b, c: (rows(b, c), 0)),
        out_shape=jax.ShapeDtypeStruct((t, SSD_D_INNER), BF16),
        scratch_shapes=[
            pltpu.VMEM((SSD_D_STATE, SSD_D_INNER), F32),
            pltpu.VMEM((8, SSD_D_INNER), F32),
            pltpu.VMEM((8, 2 * SSD_GN), F32),
        ],
        compiler_params=pltpu.CompilerParams(
            dimension_semantics=("arbitrary", "arbitrary"), vmem_limit_bytes=VMEM_LIMIT),
        name="ssd",
    )(proj, proj, proj, proj, dt_raw, cw, cb, dtb, alog, dsk, nw, expand)


def _mix_kernel(yn_ref, att_ref, gs_ref, ga_ref, x_ref, wssd_ref, wattn_ref, wout_ref, nw_ref, o_ref):
    y_ssd = _dot(yn_ref[...], wssd_ref[...])
    y_attn = _dot(att_ref[...], wattn_ref[...])
    mixed = _sigmoid(gs_ref[...].astype(F32)) * y_ssd + _sigmoid(ga_ref[...].astype(F32)) * y_attn
    o = _dot(mixed.astype(BF16), wout_ref[...])
    o_ref[...] = x_ref[...] + _rms(o, nw_ref[...])


def _mix(ynorm, att, proj, x2, wssd, wattn, wout, nw, tm=512):
    t = x2.shape[0]
    full = lambda shape: pl.BlockSpec(shape, lambda i: (0, 0))
    return pl.pallas_call(
        _mix_kernel,
        grid=(t // tm,),
        in_specs=[
            pl.BlockSpec((tm, SSD_D_INNER), lambda i: (i, 0)),
            pl.BlockSpec((tm, ATTN_WIDTH), lambda i: (i, 0)),
            pl.BlockSpec((tm, D_MODEL), lambda i: (i, GS_OFF // D_MODEL)),
            pl.BlockSpec((tm, D_MODEL), lambda i: (i, GA_OFF // D_MODEL)),
            pl.BlockSpec((tm, D_MODEL), lambda i: (i, 0)),
            full((SSD_D_INNER, D_MODEL)),
            full((ATTN_WIDTH, D_MODEL)),
            full((D_MODEL, D_MODEL)),
            full((1, D_MODEL)),
        ],
        out_specs=pl.BlockSpec((tm, D_MODEL), lambda i: (i, 0)),
        out_shape=jax.ShapeDtypeStruct((t, D_MODEL), F32),
        compiler_params=pltpu.CompilerParams(
            dimension_semantics=("arbitrary",), vmem_limit_bytes=VMEM_LIMIT),
        name="mix",
    )(ynorm, att, proj, proj, x2, wssd, wattn, wout, nw)


def _ffn_kernel(x_ref, nw1_ref, wg_ref, wu_ref, cw_ref, cb_ref, wd_ref, nw2_ref, o_ref,
                h_ref, acc_ref, carry_ref, *, tiles_per_seq):
    i = pl.program_id(0)
    k = pl.program_id(1)
    tm = x_ref.shape[0]

    @pl.when((i == 0) & (k == 0))
    def _():
        carry_ref[...] = jnp.zeros_like(carry_ref)

    @pl.when(k == 0)
    def _():
        h_ref[...] = _rms(x_ref[...], nw1_ref[...]).astype(BF16)
        acc_ref[...] = jnp.zeros_like(acc_ref)

    h = h_ref[...]
    gate = _dot(h, wg_ref[...])
    up = _dot(h, wu_ref[...])
    seq_start = (i % tiles_per_seq) == 0
    carry = jnp.where(seq_start, 0.0, carry_ref[k])
    carry_ref[k] = gate[tm - 8:, :]
    cw = cw_ref[...]
    conv = cw[FFN_CONV - 1:FFN_CONV, :] * gate + cb_ref[...]
    for s in range(1, FFN_CONV):
        conv = conv + cw[FFN_CONV - 1 - s:FFN_CONV - s, :] * _shift_rows(gate, s, carry)
    c0 = 0.7978845608028654
    gelu = 0.5 * conv * (1.0 + jnp.tanh(c0 * (conv + 0.044715 * (conv * conv * conv))))
    acc_ref[...] += _dot((gelu * up).astype(BF16), wd_ref[...])

    @pl.when(k == pl.num_programs(1) - 1)
    def _():
        o_ref[...] = x_ref[...] + _rms(acc_ref[...], nw2_ref[...])


def _ffn(x1, nw1, wup, cw, cb, wd, nw2, seq, tm=1024, tk=512):
    t = x1.shape[0]
    nk = FFN_HIDDEN // tk
    kern = functools.partial(_ffn_kernel, tiles_per_seq=seq // tm)
    return pl.pallas_call(
        kern,
        grid=(t // tm, nk),
        in_specs=[
            pl.BlockSpec((tm, D_MODEL), lambda i, k: (i, 0)),
            pl.BlockSpec((1, D_MODEL), lambda i, k: (0, 0)),
            pl.BlockSpec((D_MODEL, tk), lambda i, k: (0, k)),
            pl.BlockSpec((D_MODEL, tk), lambda i, k: (0, nk + k)),
            pl.BlockSpec((FFN_CONV, tk), lambda i, k: (0, k)),
            pl.BlockSpec((1, tk), lambda i, k: (0, k)),
            pl.BlockSpec((tk, D_MODEL), lambda i, k: (k, 0)),
            pl.BlockSpec((1, D_MODEL), lambda i, k: (0, 0)),
        ],
        out_specs=pl.BlockSpec((tm, D_MODEL), lambda i, k: (i, 0)),
        out_shape=jax.ShapeDtypeStruct((t, D_MODEL), F32),
        scratch_shapes=[
            pltpu.VMEM((tm, D_MODEL), BF16),
            pltpu.VMEM((tm, D_MODEL), F32),
            pltpu.VMEM((nk, 8, tk), F32),
        ],
        compiler_params=pltpu.CompilerParams(
            dimension_semantics=("arbitrary", "arbitrary"), vmem_limit_bytes=VMEM_LIMIT),
        name="ffn",
    )(x1, nw1, wup, wup, cw, cb, wd, nw2)


def _pad_lanes(v):
    return jnp.pad(v.astype(F32), (0, LANES - v.shape[0]))[None, :]


def kernel(x, pre_mix_norm, w_in, ssd_conv_w, ssd_conv_b, ssd_dt_bias, ssd_a_log, ssd_d_skip, ssd_out_norm,
           w_ssd_branch, w_attn_branch, w_out, post_mix_norm, pre_ffn_norm, w_ffn_up, ffn_conv_w, ffn_conv_b,
           w_ffn_down, post_ffn_norm):
    batch, seq, d = x.shape
    assert d == D_MODEL and seq % 1024 == 0
    depth = w_in.shape[0]
    x2 = x.reshape(batch * seq, d)

    slopes = 2.0 ** (-8.0 * jnp.arange(1, ATTN_N_HEADS + 1, dtype=F32) / ATTN_N_HEADS)
    head_of_channel = jnp.arange(SSD_D_INNER) // SSD_HEAD_DIM
    expand = (jnp.arange(LANES)[:, None] == head_of_channel[None, :]).astype(BF16)
    st = _IN_STARTS

    for li in range(depth):
        w = w_in[li]
        wp = jnp.concatenate([w[:, st[0]:st[2]], w[:, st[3]:st[5]], w[:, st[6]:]], axis=1).astype(BF16)
        wdt = jnp.pad(w[:, st[2]:st[3]], ((0, 0), (0, LANES - SSD_N_HEADS))).astype(BF16)
        wvt = w[:, st[5]:st[6]].T.astype(BF16)
        nw_mix = pre_mix_norm[li][None, :]

        proj, dt_raw = _inproj(x2, nw_mix, wp, wdt)
        vt3 = _vproj_t(x2, nw_mix, wvt)
        att = _moba(slopes, proj, vt3, batch, seq)
        ynorm = _ssd(proj, dt_raw, ssd_conv_w[li], ssd_conv_b[li][None, :], _pad_lanes(ssd_dt_bias[li]),
                     _pad_lanes(ssd_a_log[li]), jnp.repeat(ssd_d_skip[li].astype(F32), SSD_HEAD_DIM)[None, :],
                     ssd_out_norm[li][None, :], expand, batch, seq)
        x1 = _mix(ynorm, att, proj, x2, w_ssd_branch[li].astype(BF16), w_attn_branch[li].astype(BF16),
                  w_out[li].astype(BF16), post_mix_norm[li][None, :])
        x2 = _ffn(x1, pre_ffn_norm[li][None, :], w_ffn_up[li].astype(BF16), ffn_conv_w[li],
                  ffn_conv_b[li][None, :], w_ffn_down[li].astype(BF16), post_ffn_norm[li][None, :], seq)
    return x2.reshape(batch, seq, d)
```

```python
import functools

import jax
import jax.numpy as jnp
from jax import lax
from jax.experimental import pallas as pl
from jax.experimental.pallas import tpu as pltpu

F32 = jnp.float32
BF16 = jnp.bfloat16

D_MODEL = 1024
SSD_D_INNER = 2048
SSD_HEAD_DIM = 64
SSD_N_HEADS = 32
SSD_N_GROUPS = 4
SSD_D_STATE = 128
SSD_CONV = 4
SSD_CHUNK = 128
SSD_GN = SSD_N_GROUPS * SSD_D_STATE
SSD_GROUP_W = SSD_D_INNER // SSD_N_GROUPS
ATTN_HEAD_DIM = 64
ATTN_N_HEADS = 16
ATTN_WIDTH = 1024
MOBA_BLOCK = 256
MOBA_TOPK = 3
MOBA_RING = 6
MOBA_AHEAD = 2
MOBA_TILES_PER_STEP = 4
FFN_HIDDEN = 4096
FFN_CONV = 3
NORM_EPS = 1e-6

LANES = 128
HEADS_PER_LANE_TILE = LANES // ATTN_HEAD_DIM
NEG_BIG = -1e30
LOG2E = 1.4426950408889634

_IN_SIZES = (SSD_D_INNER, SSD_D_INNER + 2 * SSD_GN, SSD_N_HEADS, ATTN_WIDTH, ATTN_WIDTH, ATTN_WIDTH,
             D_MODEL, D_MODEL)
_IN_STARTS = tuple(sum(_IN_SIZES[:j]) for j in range(len(_IN_SIZES)))
Z_OFF = 0
XS_OFF = 2048
B_OFF = 4096
C_OFF = 4608
Q_OFF = 5120
K_OFF = 6144
GS_OFF = 7168
GA_OFF = 8192
PROJ_COLS = 9216

VMEM_LIMIT = 48 * 1024 * 1024


def _dot(a, b):
    return jnp.dot(a, b, preferred_element_type=F32)


def _dot_nt(a, b):
    return lax.dot_general(a, b, (((1,), (1,)), ((), ())), preferred_element_type=F32)


def _rms(x, w):
    return x * lax.rsqrt(jnp.mean(x * x, axis=-1, keepdims=True) + NORM_EPS) * w


def _sigmoid(x):
    return 1.0 / (1.0 + jnp.exp(-x))


def _silu(x):
    return x * _sigmoid(x)


def _split3(v):
    hi = v.astype(BF16)
    r1 = v - hi.astype(F32)
    mid = r1.astype(BF16)
    lo = (r1 - mid.astype(F32)).astype(BF16)
    return hi, mid, lo


def _shift_rows(x, k, carry):
    r = pltpu.roll(x, k, axis=0)
    hist = carry.shape[0]
    row = lax.broadcasted_iota(jnp.int32, carry.shape, 0)
    top = jnp.where(row < k, pltpu.roll(carry, k, axis=0), r[:hist])
    return jnp.concatenate([top, r[hist:]], axis=0)


def _inproj_kernel(x_ref, nw_ref, w_ref, wdt_ref, proj_ref, dt_ref, h_ref):
    @pl.when(pl.program_id(1) == 0)
    def _():
        h = _rms(x_ref[...], nw_ref[...]).astype(BF16)
        h_ref[...] = h
        dt_ref[...] = _dot(h, wdt_ref[...])

    proj_ref[...] = _dot(h_ref[...], w_ref[...]).astype(BF16)


def _inproj(x2, nw, wp, wdt, tm=1024, tn=3072):
    t = x2.shape[0]
    return pl.pallas_call(
        _inproj_kernel,
        grid=(t // tm, PROJ_COLS // tn),
        in_specs=[
            pl.BlockSpec((tm, D_MODEL), lambda i, j: (i, 0)),
            pl.BlockSpec((1, D_MODEL), lambda i, j: (0, 0)),
            pl.BlockSpec((D_MODEL, tn), lambda i, j: (0, j)),
            pl.BlockSpec((D_MODEL, LANES), lambda i, j: (0, 0)),
        ],
        out_specs=[
            pl.BlockSpec((tm, tn), lambda i, j: (i, j)),
            pl.BlockSpec((tm, LANES), lambda i, j: (i, 0)),
        ],
        out_shape=[
            jax.ShapeDtypeStruct((t, PROJ_COLS), BF16),
            jax.ShapeDtypeStruct((t, LANES), F32),
        ],
        scratch_shapes=[pltpu.VMEM((tm, D_MODEL), BF16)],
        compiler_params=pltpu.CompilerParams(
            dimension_semantics=("arbitrary", "arbitrary"), vmem_limit_bytes=VMEM_LIMIT),
        name="inproj",
    )(x2, nw, wp, wdt)


def _vt_kernel(x_ref, nw_ref, wvt_ref, vt_ref):
    h = _rms(x_ref[...], nw_ref[...]).astype(BF16)
    vt = _dot_nt(wvt_ref[...], h)
    for c in range(vt_ref.shape[0]):
        vt_ref[c] = vt[:, c * MOBA_BLOCK:(c + 1) * MOBA_BLOCK].astype(BF16)


def _vproj_t(x2, nw, wvt, tm=512):
    t = x2.shape[0]
    return pl.pallas_call(
        _vt_kernel,
        grid=(t // tm,),
        in_specs=[
            pl.BlockSpec((tm, D_MODEL), lambda i: (i, 0)),
            pl.BlockSpec((1, D_MODEL), lambda i: (0, 0)),
            pl.BlockSpec((ATTN_WIDTH, D_MODEL), lambda i: (0, 0)),
        ],
        out_specs=pl.BlockSpec((tm // MOBA_BLOCK, ATTN_WIDTH, MOBA_BLOCK), lambda i: (i, 0, 0)),
        out_shape=jax.ShapeDtypeStruct((t // MOBA_BLOCK, ATTN_WIDTH, MOBA_BLOCK), BF16),
        compiler_params=pltpu.CompilerParams(
            dimension_semantics=("arbitrary",), vmem_limit_bytes=VMEM_LIMIT),
        name="vproj_t",
    )(x2, nw, wvt)


def _moba_kernel(*refs):
    for u in range(MOBA_TILES_PER_STEP):
        _moba_tile(u, *refs)


def _moba_tile(u, slopes_ref, q_ref, k_ref, vt_ref, o_ref, km_ref, kh_ref, vts_ref, sel_ref, *ring_refs):
    hp = pl.program_id(1)
    i = pl.program_id(2) * MOBA_TILES_PER_STEP + u
    nb = km_ref.shape[0]
    blk = MOBA_BLOCK
    rows_u = slice(u * blk, (u + 1) * blk)
    heads = range(HEADS_PER_LANE_TILE)
    den_rows = 16
    vt_rows = ATTN_HEAD_DIM + den_rows

    if u == 0:
        @pl.when(i == 0)
        def _():
            km_ref[...] = jnp.zeros_like(km_ref)

    q_raw = q_ref[rows_u, :]
    qs = (q_raw.astype(F32) * (ATTN_HEAD_DIM ** -0.5 * LOG2E)).astype(BF16)
    lane = lax.broadcasted_iota(jnp.int32, (1, LANES), 1)
    key_pos = lax.broadcasted_iota(jnp.int32, (blk, blk), 0)
    qry_pos = lax.broadcasted_iota(jnp.int32, (blk, blk), 1)
    key_lane = lax.broadcasted_iota(jnp.int32, (1, blk), 1)
    blk_id = lax.broadcasted_iota(jnp.int32, (nb, blk), 0)
    k_own = k_ref[rows_u, :]
    vt_own = vt_ref[u]
    valid = blk_id < i

    head_lanes, vrows, slope = [], [], []
    for hh in heads:
        lo_lane = hh * ATTN_HEAD_DIM
        head_lanes.append((lane >= lo_lane) & (lane < lo_lane + ATTN_HEAD_DIM))
        vrows.append(slice(lo_lane, lo_lane + ATTN_HEAD_DIM))
        slope.append(slopes_ref[hp * HEADS_PER_LANE_TILE + hh] * LOG2E)

    kh_own = jnp.concatenate([jnp.where(head_lanes[hh], k_own, jnp.zeros_like(k_own)) for hh in heads], axis=0)
    kh_ref[i] = kh_own

    km_ref[pl.ds(i, 1), :] = jnp.mean(k_own.astype(F32), axis=0, keepdims=True)
    km = km_ref[...]
    gate_lhs = jnp.concatenate(
        [part for hh in heads for part in _split3(jnp.where(head_lanes[hh], km, 0.0))], axis=0)
    gate_all = _dot_nt(gate_lhs, q_raw)
    for hh in heads:
        gate = sum(gate_all[(3 * hh + c) * nb:(3 * hh + c + 1) * nb] for c in range(3))
        g = jnp.where(valid, gate, -jnp.inf)
        sel = jnp.zeros((nb, blk), jnp.bool_)
        for _ in range(MOBA_TOPK):
            top = jnp.max(g, axis=0, keepdims=True)
            first = jnp.min(jnp.where(g == top, blk_id, nb), axis=0, keepdims=True)
            pick = blk_id == first
            sel = sel | (pick & (top > -jnp.inf))
            g = jnp.where(pick, -jnp.inf, g)
        sel_ref[hh, :nb, :] = sel.astype(F32)
        sel_ref[hh, nb:, :] = jnp.zeros((sel_ref.shape[1] - nb, blk), F32)

    def store_scores(j, dst_ref):
        dst_ref[...] = _dot_nt(kh_ref[jnp.minimum(j, i)], qs)

    row0 = lax.broadcasted_iota(jnp.int32, (den_rows, blk), 0) == 0
    own_slot = nb + u
    for hh in heads:
        w = jnp.exp2(slope[hh] * (key_lane - (blk - 1)).astype(F32))
        v_rows = slice(hh * vt_rows, hh * vt_rows + ATTN_HEAD_DIM)
        d_rows = slice(hh * vt_rows + ATTN_HEAD_DIM, (hh + 1) * vt_rows)
        vts_ref[i, v_rows, :] = (vt_own[vrows[hh], :].astype(F32) * w).astype(BF16)
        vts_ref[i, d_rows, :] = jnp.where(row0, w, 0.0).astype(BF16)
        vts_ref[own_slot, v_rows, :] = vt_own[vrows[hh], :]
        vts_ref[own_slot, d_rows, :] = jnp.where(row0, 1.0, 0.0).astype(BF16)

    def softmax_part(src_ref, dst_ref, j, carry):
        out = list(carry)
        for hh in heads:
            m = carry[4 * hh]
            s = src_ref[hh * blk:(hh + 1) * blk, :]
            c_j = slope[hh] * (blk * (j - i) + (blk - 1)).astype(F32)
            chosen = sel_ref[hh, pl.ds(j, 1), :] > 0.5
            m_new = jnp.where(chosen, jnp.maximum(m, jnp.max(s, axis=0, keepdims=True) + c_j), m)
            p = jnp.exp2(s - jnp.where(chosen, m_new - c_j, -NEG_BIG))
            dst_ref[hh * blk:(hh + 1) * blk, :] = p.astype(BF16)
            out[4 * hh] = m_new
            out[4 * hh + 3] = jnp.exp2(m - m_new)
        return tuple(out)

    def value_part(src_ref, j, carry):
        slot = jnp.where(j < 0, own_slot, jnp.minimum(j, i))
        out = list(carry)
        for hh in heads:
            _, l, acc, alpha = carry[4 * hh:4 * hh + 4]
            r = _dot(vts_ref[slot, hh * vt_rows:(hh + 1) * vt_rows, :],
                     src_ref[hh * blk:(hh + 1) * blk, :])
            out[4 * hh + 1] = alpha * l + r[ATTN_HEAD_DIM:ATTN_HEAD_DIM + 1]
            out[4 * hh + 2] = alpha * acc + r[:ATTN_HEAD_DIM]
        return tuple(out)

    ring = MOBA_RING
    sc = ring_refs[:ring]
    pb = ring_refs[ring:]
    for c in range(MOBA_AHEAD):
        store_scores(c, sc[c])

    s_own = _dot_nt(kh_own, qs)
    col_bias = key_pos.astype(F32)
    state = []
    for hh in heads:
        s = s_own[hh * blk:(hh + 1) * blk] + slope[hh] * col_bias
        s = jnp.where(key_pos <= qry_pos, s, NEG_BIG)
        m = jnp.max(s, axis=0, keepdims=True)
        pb[ring - 1][hh * blk:(hh + 1) * blk, :] = jnp.exp2(s - m).astype(BF16)
        state += [m, jnp.zeros_like(m), jnp.zeros((ATTN_HEAD_DIM, blk), F32), jnp.ones_like(m)]

    def body(t, carry):
        for c in range(ring):
            k = ring * t + c
            store_scores(k + MOBA_AHEAD, sc[(c + MOBA_AHEAD) % ring])
            carry = value_part(pb[(c - 1) % ring], k - 1, carry)
            carry = softmax_part(sc[c], pb[c], k, carry)
        return carry

    trips = (i + ring - 1) // ring
    state = lax.fori_loop(0, trips, body, tuple(state))
    state = value_part(pb[ring - 1], ring * trips - 1, state)
    outs = [state[4 * hh + 2] / state[4 * hh + 1] for hh in heads]
    o_ref[rows_u, :] = jnp.concatenate(outs, axis=0).T.astype(o_ref.dtype)


def _moba(slopes, proj, vt3, batch, seq):
    t = proj.shape[0]
    nq = seq // MOBA_BLOCK
    n_pairs = ATTN_N_HEADS // HEADS_PER_LANE_TILE
    q_blk0 = Q_OFF // LANES
    k_blk0 = K_OFF // LANES
    stacked = HEADS_PER_LANE_TILE * MOBA_BLOCK
    per_step = MOBA_TILES_PER_STEP
    ns = nq // per_step
    rows = per_step * MOBA_BLOCK
    return pl.pallas_call(
        _moba_kernel,
        grid=(batch, n_pairs, ns),
        in_specs=[
            pl.BlockSpec(memory_space=pltpu.SMEM),
            pl.BlockSpec((rows, LANES), lambda b, hp, i: (b * ns + i, q_blk0 + hp)),
            pl.BlockSpec((rows, LANES), lambda b, hp, i: (b * ns + i, k_blk0 + hp)),
            pl.BlockSpec((per_step, LANES, MOBA_BLOCK), lambda b, hp, i: (b * ns + i, hp, 0)),
        ],
        out_specs=pl.BlockSpec((rows, LANES), lambda b, hp, i: (b * ns + i, hp)),
        out_shape=jax.ShapeDtypeStruct((t, ATTN_WIDTH), BF16),
        scratch_shapes=[
            pltpu.VMEM((nq, LANES), F32),
            pltpu.VMEM((nq, stacked, LANES), BF16),
            pltpu.VMEM((nq + per_step, HEADS_PER_LANE_TILE * (ATTN_HEAD_DIM + 16), MOBA_BLOCK), BF16),
            pltpu.VMEM((HEADS_PER_LANE_TILE, nq + 8, MOBA_BLOCK), F32),
        ] + [pltpu.VMEM((stacked, MOBA_BLOCK), F32)] * MOBA_RING + [pltpu.VMEM((stacked, MOBA_BLOCK), BF16)] * MOBA_RING,
        compiler_params=pltpu.CompilerParams(
            dimension_semantics=("arbitrary", "arbitrary", "arbitrary"), vmem_limit_bytes=VMEM_LIMIT),
        name="moba",
    )(slopes, proj, proj, vt3)


def _conv_silu(x, carry, w, b):
    y = w[3:4, :] * x + b
    for k in range(1, SSD_CONV):
        y = y + w[SSD_CONV - 1 - k:SSD_CONV - k, :] * _shift_rows(x, k, carry)
    return _silu(y)


def _ssd_kernel(z_ref, xs_ref, b_ref, c_ref, dt_ref, cw_ref, cb_ref, dtb_ref, alog_ref, dsk_ref, nw_ref,
                e_ref, y_ref, st_ref, cx_ref, cbc_ref):
    L = SSD_CHUNK

    @pl.when(pl.program_id(1) == 0)
    def _():
        st_ref[...] = jnp.zeros_like(st_ref)
        cx_ref[...] = jnp.zeros_like(cx_ref)
        cbc_ref[...] = jnp.zeros_like(cbc_ref)

    def conv_silu(src_ref, hist_ref, cols, hist_cols, w_cols):
        raw = src_ref[:, cols].astype(F32)
        out = _conv_silu(raw, hist_ref[:, hist_cols], cw_ref[:, w_cols], cb_ref[:, w_cols])
        hist_ref[:, hist_cols] = raw[L - 8:, :]
        return out

    xdt_in = dt_ref[...] + dtb_ref[...]
    dt = jnp.maximum(xdt_in, 0.0) + jnp.log1p(jnp.exp(-jnp.abs(xdt_in)))
    adt = dt * (-jnp.exp(alog_ref[...]))
    row = lax.broadcasted_iota(jnp.int32, (L, L), 0)
    col = lax.broadcasted_iota(jnp.int32, (L, L), 1)
    causal = row >= col
    tri = causal.astype(BF16)
    a_hi, a_mid, a_lo = _split3(adt)
    acs = _dot(tri, a_hi) + _dot(tri, a_mid) + _dot(tri, a_lo)
    acs_t = acs.T
    eacs = jnp.exp(acs)
    wend = jnp.exp(acs[L - 1:L, :] - acs)

    stacked = jnp.concatenate([dt, eacs, wend], axis=0)
    s_parts = _split3(stacked)
    lane = lax.broadcasted_iota(jnp.int32, (L, LANES), 1)
    first_head = lane < SSD_HEAD_DIM

    for g in range(SSD_N_GROUPS):
        gl = slice(g * SSD_GROUP_W, (g + 1) * SSD_GROUP_W)
        nl = slice(g * SSD_D_STATE, (g + 1) * SSD_D_STATE)
        cl = slice(SSD_GN + g * SSD_D_STATE, SSD_GN + (g + 1) * SSD_D_STATE)
        e = e_ref[:, gl]
        expanded = sum(_dot(part, e) for part in s_parts)
        dt_x = expanded[:L]
        eacs_x = expanded[L:2 * L]
        wend_x = expanded[2 * L:]

        xs = conv_silu(xs_ref, cx_ref, gl, gl, gl)
        bg = conv_silu(b_ref, cbc_ref, nl, nl, slice(SSD_D_INNER + nl.start, SSD_D_INNER + nl.stop))
        cg = conv_silu(c_ref, cbc_ref, nl, cl, slice(SSD_D_INNER + cl.start, SSD_D_INNER + cl.stop)).astype(BF16)
        xdt = xs * dt_x
        xdt_b = xdt.astype(BF16)
        xw_b = (xdt * wend_x).astype(BF16)

        cb = _dot_nt(cg, bg.astype(BF16))
        y_inter = _dot(cg, st_ref[:, gl].astype(BF16))
        pairs = []
        for pr in range(SSD_GROUP_W // LANES):
            h0 = (g * SSD_GROUP_W + pr * LANES) // SSD_HEAD_DIM
            xp = xdt_b[:, pr * LANES:(pr + 1) * LANES]
            ys = []
            for h in (h0, h0 + 1):
                seg = acs[:, h:h + 1] - acs_t[h:h + 1, :]
                decay = jnp.exp(jnp.where(causal, seg, NEG_BIG))
                ys.append(_dot((cb * decay).astype(BF16), xp))
            pairs.append(jnp.where(first_head, ys[0], ys[1]))
        y = jnp.concatenate(pairs, axis=1) + y_inter * eacs_x + dsk_ref[:, gl] * xs
        st_ref[:, gl] = st_ref[:, gl] * eacs_x[L - 1:L, :] + _dot(bg.T.astype(BF16), xw_b)
        y = y * _silu(z_ref[:, gl].astype(F32))
        y_ref[:, gl] = _rms(y, nw_ref[:, gl]).astype(y_ref.dtype)


def _ssd(proj, dt_raw, cw, cb, dtb, alog, dsk, nw, expand, batch, seq):
    t = proj.shape[0]
    nc = seq // SSD_CHUNK
    L = SSD_CHUNK
    conv_dim = SSD_D_INNER + 2 * SSD_GN

    def rows(b, c):
        return b * nc + c

    full = lambda shape: pl.BlockSpec(shape, lambda b, c: (0, 0))
    return pl.pallas_call(
        _ssd_kernel,
        grid=(batch, nc),
        in_specs=[
            pl.BlockSpec((L, SSD_D_INNER), lambda b, c: (rows(b, c), Z_OFF // SSD_D_INNER)),
            pl.BlockSpec((L, SSD_D_INNER), lambda b, c: (rows(b, c), XS_OFF // SSD_D_INNER)),
            pl.BlockSpec((L, SSD_GN), lambda b, c: (rows(b, c), B_OFF // SSD_GN)),
            pl.BlockSpec((L, SSD_GN), lambda b, c: (rows(b, c), C_OFF // SSD_GN)),
            pl.BlockSpec((L, LANES), lambda b, c: (rows(b, c), 0)),
            full((SSD_CONV, conv_dim)),
            full((1, conv_dim)),
            full((1, LANES)),
            full((1, LANES)),
            full((1, SSD_D_INNER)),
            full((1, SSD_D_INNER)),
            full((LANES, SSD_D_INNER)),
        ],
        out_specs=pl.BlockSpec((L, SSD_D_INNER), lambda b, c: (rows(b, c), 0)),
        out_shape=jax.ShapeDtypeStruct((t, SSD_D_INNER), BF16),
        scratch_shapes=[
            pltpu.VMEM((SSD_D_STATE, SSD_D_INNER), F32),
            pltpu.VMEM((8, SSD_D_INNER), F32),
            pltpu.VMEM((8, 2 * SSD_GN), F32),
        ],
        compiler_params=pltpu.CompilerParams(
            dimension_semantics=("arbitrary", "arbitrary"), vmem_limit_bytes=VMEM_LIMIT),
        name="ssd",
    )(proj, proj, proj, proj, dt_raw, cw, cb, dtb, alog, dsk, nw, expand)


def _mix_kernel(yn_ref, att_ref, gs_ref, ga_ref, x_ref, wssd_ref, wattn_ref, wout_ref, nw_ref, o_ref):
    y_ssd = _dot(yn_ref[...], wssd_ref[...])
    y_attn = _dot(att_ref[...], wattn_ref[...])
    mixed = _sigmoid(gs_ref[...].astype(F32)) * y_ssd + _sigmoid(ga_ref[...].astype(F32)) * y_attn
    o = _dot(mixed.astype(BF16), wout_ref[...])
    o_ref[...] = x_ref[...] + _rms(o, nw_ref[...])


def _mix(ynorm, att, proj, x2, wssd, wattn, wout, nw, tm=512):
    t = x2.shape[0]
    full = lambda shape: pl.BlockSpec(shape, lambda i: (0, 0))
    return pl.pallas_call(
        _mix_kernel,
        grid=(t // tm,),
        in_specs=[
            pl.BlockSpec((tm, SSD_D_INNER), lambda i: (i, 0)),
            pl.BlockSpec((tm, ATTN_WIDTH), lambda i: (i, 0)),
            pl.BlockSpec((tm, D_MODEL), lambda i: (i, GS_OFF // D_MODEL)),
            pl.BlockSpec((tm, D_MODEL), lambda i: (i, GA_OFF // D_MODEL)),
            pl.BlockSpec((tm, D_MODEL), lambda i: (i, 0)),
            full((SSD_D_INNER, D_MODEL)),
            full((ATTN_WIDTH, D_MODEL)),
            full((D_MODEL, D_MODEL)),
            full((1, D_MODEL)),
        ],
        out_specs=pl.BlockSpec((tm, D_MODEL), lambda i: (i, 0)),
        out_shape=jax.ShapeDtypeStruct((t, D_MODEL), F32),
        compiler_params=pltpu.CompilerParams(
            dimension_semantics=("arbitrary",), vmem_limit_bytes=VMEM_LIMIT),
        name="mix",
    )(ynorm, att, proj, proj, x2, wssd, wattn, wout, nw)


def _ffn_kernel(x_ref, nw1_ref, wg_ref, wu_ref, cw_ref, cb_ref, wd_ref, nw2_ref, o_ref,
                h_ref, acc_ref, carry_ref, *, tiles_per_seq):
    i = pl.program_id(0)
    k = pl.program_id(1)
    tm = x_ref.shape[0]

    @pl.when((i == 0) & (k == 0))
    def _():
        carry_ref[...] = jnp.zeros_like(carry_ref)

    @pl.when(k == 0)
    def _():
        h_ref[...] = _rms(x_ref[...], nw1_ref[...]).astype(BF16)
        acc_ref[...] = jnp.zeros_like(acc_ref)

    h = h_ref[...]
    gate = _dot(h, wg_ref[...])
    up = _dot(h, wu_ref[...])
    seq_start = (i % tiles_per_seq) == 0
    carry = jnp.where(seq_start, 0.0, carry_ref[k])
    carry_ref[k] = gate[tm - 8:, :]
    cw = cw_ref[...]
    conv = cw[FFN_CONV - 1:FFN_CONV, :] * gate + cb_ref[...]
    for s in range(1, FFN_CONV):
        conv = conv + cw[FFN_CONV - 1 - s:FFN_CONV - s, :] * _shift_rows(gate, s, carry)
    c0 = 0.7978845608028654
    gelu = 0.5 * conv * (1.0 + jnp.tanh(c0 * (conv + 0.044715 * (conv * conv * conv))))
    acc_ref[...] += _dot((gelu * up).astype(BF16), wd_ref[...])

    @pl.when(k == pl.num_programs(1) - 1)
    def _():
        o_ref[...] = x_ref[...] + _rms(acc_ref[...], nw2_ref[...])


def _ffn(x1, nw1, wup, cw, cb, wd, nw2, seq, tm=1024, tk=1024):
    t = x1.shape[0]
    nk = FFN_HIDDEN // tk
    kern = functools.partial(_ffn_kernel, tiles_per_seq=seq // tm)
    return pl.pallas_call(
        kern,
        grid=(t // tm, nk),
        in_specs=[
            pl.BlockSpec((tm, D_MODEL), lambda i, k: (i, 0)),
            pl.BlockSpec((1, D_MODEL), lambda i, k: (0, 0)),
            pl.BlockSpec((D_MODEL, tk), lambda i, k: (0, k)),
            pl.BlockSpec((D_MODEL, tk), lambda i, k: (0, nk + k)),
            pl.BlockSpec((FFN_CONV, tk), lambda i, k: (0, k)),
            pl.BlockSpec((1, tk), lambda i, k: (0, k)),
            pl.BlockSpec((tk, D_MODEL), lambda i, k: (k, 0)),
            pl.BlockSpec((1, D_MODEL), lambda i, k: (0, 0)),
        ],
        out_specs=pl.BlockSpec((tm, D_MODEL), lambda i, k: (i, 0)),
        out_shape=jax.ShapeDtypeStruct((t, D_MODEL), F32),
        scratch_shapes=[
            pltpu.VMEM((tm, D_MODEL), BF16),
            pltpu.VMEM((tm, D_MODEL), F32),
            pltpu.VMEM((nk, 8, tk), F32),
        ],
        compiler_params=pltpu.CompilerParams(
            dimension_semantics=("arbitrary", "arbitrary"), vmem_limit_bytes=VMEM_LIMIT),
        name="ffn",
    )(x1, nw1, wup, wup, cw, cb, wd, nw2)


def _pad_lanes(v):
    return jnp.pad(v.astype(F32), (0, LANES - v.shape[0]))[None, :]


def kernel(x, pre_mix_norm, w_in, ssd_conv_w, ssd_conv_b, ssd_dt_bias, ssd_a_log, ssd_d_skip, ssd_out_norm,
           w_ssd_branch, w_attn_branch, w_out, post_mix_norm, pre_ffn_norm, w_ffn_up, ffn_conv_w, ffn_conv_b,
           w_ffn_down, post_ffn_norm):
    batch, seq, d = x.shape
    assert d == D_MODEL and seq % 1024 == 0
    depth = w_in.shape[0]
    x2 = x.reshape(batch * seq, d)

    slopes = 2.0 ** (-8.0 * jnp.arange(1, ATTN_N_HEADS + 1, dtype=F32) / ATTN_N_HEADS)
    head_of_channel = jnp.arange(SSD_D_INNER) // SSD_HEAD_DIM
    expand = (jnp.arange(LANES)[:, None] == head_of_channel[None, :]).astype(BF16)
    st = _IN_STARTS

    for li in range(depth):
        w = w_in[li]
        wp = jnp.concatenate([w[:, st[0]:st[2]], w[:, st[3]:st[5]], w[:, st[6]:]], axis=1).astype(BF16)
        wdt = jnp.pad(w[:, st[2]:st[3]], ((0, 0), (0, LANES - SSD_N_HEADS))).astype(BF16)
        wvt = w[:, st[5]:st[6]].T.astype(BF16)
        nw_mix = pre_mix_norm[li][None, :]

        proj, dt_raw = _inproj(x2, nw_mix, wp, wdt)
        vt3 = _vproj_t(x2, nw_mix, wvt)
        att = _moba(slopes, proj, vt3, batch, seq)
        ynorm = _ssd(proj, dt_raw, ssd_conv_w[li], ssd_conv_b[li][None, :], _pad_lanes(ssd_dt_bias[li]),
                     _pad_lanes(ssd_a_log[li]), jnp.repeat(ssd_d_skip[li].astype(F32), SSD_HEAD_DIM)[None, :],
                     ssd_out_norm[li][None, :], expand, batch, seq)
        x1 = _mix(ynorm, att, proj, x2, w_ssd_branch[li].astype(BF16), w_attn_branch[li].astype(BF16),
                  w_out[li].astype(BF16), post_mix_norm[li][None, :])
        x2 = _ffn(x1, pre_ffn_norm[li][None, :], w_ffn_up[li].astype(BF16), ffn_conv_w[li],
                  ffn_conv_b[li][None, :], w_ffn_down[li].astype(BF16), post_ffn_norm[li][None, :], seq)
    return x2.reshape(batch, seq, d)
```
